```python
import jax, jax.numpy as jnp
from jax import lax
import numpy as np

D_MODEL = 1024
BATCH = 8
SEQ = 2048
DEPTH = 1
DEC_BATCH = 128
DEC_SEQ = 4
PAST_LEN = 16384
PAGE_SIZE = 128

D_MIX = D_MODEL
D_POOL = D_MIX // 2
D_RNN = D_MIX - D_POOL
POOL_WINDOWS = (2, 4, 8, 16)
N_POOL_GROUPS = len(POOL_WINDOWS)
POOL_GROUP = D_POOL // N_POOL_GROUPS
POOL_BUF = max(POOL_WINDOWS) - 1
N_RNN_HEADS = 8
RNN_HEAD = D_RNN // N_RNN_HEADS
CONV_WIDTH = 4
LRU_C = 8.0
N_EXPERT_GROUPS = 4
EXPERTS_PER_GROUP = 4
N_EXPERTS = N_EXPERT_GROUPS * EXPERTS_PER_GROUP
D_EXPERT = 512
TOP_K = 2
D_PLE = 256
LN_EPS = 1e-5
ALPHA = (2 * DEPTH) ** 0.25
BETA = (8 * DEPTH) ** -0.25

kernel_name = "hymba_pool_rglru_hmoe_step"

F32 = jnp.float32


def layer_norm(x, g, b):
    x = x.astype(F32)
    mu = jnp.mean(x, axis=-1, keepdims=True)
    var = jnp.mean(jnp.square(x - mu), axis=-1, keepdims=True)
    return (x - mu) * lax.rsqrt(var + LN_EPS) * g + b


def pool_mixer(u, buf, pos0, w, b, scale):
    bn, t, _ = u.shape
    full = jnp.concatenate([buf.astype(F32), u], axis=1)
    csum = jnp.concatenate([jnp.zeros_like(full[:, :1]), lax.cumsum(full, axis=1)], axis=1)
    end = csum[:, POOL_BUF + 1:]
    pos = pos0 + jnp.arange(t)
    diffs = []
    for g, win in enumerate(POOL_WINDOWS):
        sl = slice(g * POOL_GROUP, (g + 1) * POOL_GROUP)
        start = csum[:, POOL_BUF + 1 - win:POOL_BUF + 1 - win + t, sl]
        cnt = jnp.minimum(pos + 1, win).astype(F32)[None, :, None]
        diffs.append((end[..., sl] - start) / cnt - u[..., sl])
    d = jnp.stack(diffs, axis=2)
    out = jnp.einsum('btgc,gcd->btgd', d, w) + b
    return out.reshape(bn, t, D_POOL) * scale, full[:, -POOL_BUF:]


def _lru_combine(left, right):
    a1, b1 = left
    a2, b2 = right
    return a1 * a2, a2 * b1 + b2


def rglru_mixer(xr, gate, conv_buf, h0, pos0, conv_w, conv_b, wa, ba, wx, bx, lam):
    bn, t, _ = xr.shape
    full = jnp.concatenate([conv_buf.astype(F32), xr], axis=1)
    xc = conv_b + full[:, 0:t] * conv_w[0]
    for k in range(1, CONV_WIDTH):
        xc = xc + full[:, k:k + t] * conv_w[k]
    xh = xc.reshape(bn, t, N_RNN_HEADS, RNN_HEAD)
    r = jax.nn.sigmoid(jnp.einsum('bthi,hij->bthj', xh, wa).reshape(bn, t, D_RNN) + ba)
    i = jax.nn.sigmoid(jnp.einsum('bthi,hij->bthj', xh, wx).reshape(bn, t, D_RNN) + bx)
    log_a = -LRU_C * r * jax.nn.softplus(-lam.astype(F32))
    a = jnp.exp(log_a)
    pos = pos0 + jnp.arange(t)
    mult = jnp.where((pos == 0)[None, :, None], 1.0, jnp.sqrt(-jnp.expm1(2.0 * log_a)))
    bterm = mult * i * xc
    bterm = bterm.at[:, 0].add(a[:, 0] * h0.astype(F32))
    _, h = lax.associative_scan(_lru_combine, (a, bterm), axis=1)
    y = h * jax.nn.gelu(gate)
    return y, full[:, -(CONV_WIDTH - 1):], h[:, -1]


def hier_moe(x2, w_rg, b_rg, w_re, b_re, w_eg, w_eu, w_ed):
    g_prob = jax.nn.softmax((x2 @ w_rg + b_rg).astype(F32), axis=-1)
    g_w, g_idx = lax.top_k(g_prob, 1)
    e_logits = (x2 @ w_re + b_re).astype(F32).reshape(-1, N_EXPERT_GROUPS, EXPERTS_PER_GROUP)
    g_sel = jax.nn.one_hot(g_idx[:, 0], N_EXPERT_GROUPS, dtype=F32)
    e_in = jnp.einsum('nge,ng->ne', e_logits, g_sel)
    e_prob = jax.nn.softmax(e_in, axis=-1)
    e_w, e_idx = lax.top_k(e_prob, TOP_K)
    e_w = e_w / jnp.sum(e_w, axis=-1, keepdims=True)
    expert_id = g_idx * EXPERTS_PER_GROUP + e_idx
    comb = jnp.sum(jax.nn.one_hot(expert_id, N_EXPERTS, dtype=F32) * (g_w * e_w)[..., None], axis=1)
    y = jnp.zeros_like(x2)
    for e in range(N_EXPERTS):
        h = jax.nn.silu(x2 @ w_eg[e]) * (x2 @ w_eu[e])
        y = y + comb[:, e:e + 1] * (h @ w_ed[e])
    return y


def decoder_layer(x, pe, pos0, pool_buf, conv_buf, h0,
                  w_in, pool_w, pool_b, pool_scale, conv_w, conv_b, rg_wa, rg_ba, rg_wx, rg_bx, rg_lambda,
                  w_out, ln1_g, ln1_b, w_rg, b_rg, w_re, b_re, w_eg, w_eu, w_ed, ln2_g, ln2_b,
                  w_pg, b_pg, w_ple, ln3_g, ln3_b):
    bn, t, _ = x.shape
    proj = jnp.einsum('btd,de->bte', x, w_in).astype(F32)
    u = proj[..., :D_POOL]
    xr = proj[..., D_POOL:D_POOL + D_RNN]
    gate = proj[..., D_POOL + D_RNN:]
    pool_out, pool_new = pool_mixer(u, pool_buf, pos0, pool_w, pool_b, pool_scale)
    rnn_out, conv_new, h_new = rglru_mixer(xr, gate, conv_buf, h0, pos0, conv_w, conv_b,
                                           rg_wa, rg_ba, rg_wx, rg_bx, rg_lambda)
    mix = jnp.einsum('bte,ed->btd', jnp.concatenate([pool_out, rnn_out], axis=-1), w_out)
    x = layer_norm(ALPHA * x + mix, ln1_g, ln1_b)
    moe = hier_moe(x.reshape(-1, D_MODEL), w_rg, b_rg, w_re, b_re, w_eg, w_eu, w_ed).reshape(bn, t, D_MODEL)
    x = layer_norm(ALPHA * x + moe, ln2_g, ln2_b)
    gate_p = jax.nn.sigmoid(jnp.einsum('btd,de->bte', x, w_pg) + b_pg)
    ple = jnp.einsum('btp,pd->btd', pe.astype(F32), w_ple) * gate_p
    x = layer_norm(ALPHA * x + ple, ln3_g, ln3_b)
    return x, pool_new, conv_new, h_new


def setup_inputs(seed: int = 0) -> dict:
    key = jax.random.key(seed)
    ks = jax.random.split(key, 40)
    nrm = lambda k, shape, s: jax.random.normal(k, shape, F32) * s
    a0 = jax.random.uniform(ks[20], (DEPTH, D_RNN), F32, minval=0.9, maxval=0.999)
    s0 = a0 ** (1.0 / LRU_C)
    d_in = D_POOL + 2 * D_RNN
    return {
        "x_prompt": nrm(ks[0], (BATCH, SEQ, D_MODEL), 1.0),
        "x_sample": nrm(ks[1], (DEC_BATCH, DEC_SEQ, D_MODEL), 1.0),
        "state_pool": nrm(ks[2], (DEPTH, DEC_BATCH, POOL_BUF, D_POOL), 1.0),
        "state_conv": nrm(ks[3], (DEPTH, DEC_BATCH, CONV_WIDTH - 1, D_RNN), 1.0),
        "state_h": nrm(ks[4], (DEPTH, DEC_BATCH, D_RNN), 0.5),
        "p_prompt": nrm(ks[5], (DEPTH, BATCH, SEQ, D_PLE), 1.0),
        "p_sample": nrm(ks[6], (DEPTH, DEC_BATCH, DEC_SEQ, D_PLE), 1.0),
        "ln_in_g": 1.0 + nrm(ks[7], (D_MODEL,), 0.02),
        "ln_in_b": nrm(ks[8], (D_MODEL,), 0.02),
        "w_in": nrm(ks[9], (DEPTH, D_MODEL, d_in), D_MODEL ** -0.5),
        "pool_w": nrm(ks[10], (DEPTH, N_POOL_GROUPS, POOL_GROUP, POOL_GROUP), POOL_GROUP ** -0.5),
        "pool_b": nrm(ks[11], (DEPTH, N_POOL_GROUPS, POOL_GROUP), 0.01),
        "pool_scale": 1.0 + nrm(ks[12], (DEPTH, D_POOL), 0.02),
        "conv_w": nrm(ks[13], (DEPTH, CONV_WIDTH, D_RNN), CONV_WIDTH ** -0.5),
        "conv_b": nrm(ks[14], (DEPTH, D_RNN), 0.01),
        "rg_wa": nrm(ks[15], (DEPTH, N_RNN_HEADS, RNN_HEAD, RNN_HEAD), RNN_HEAD ** -0.5),
        "rg_ba": nrm(ks[16], (DEPTH, D_RNN), 0.01),
        "rg_wx": nrm(ks[17], (DEPTH, N_RNN_HEADS, RNN_HEAD, RNN_HEAD), RNN_HEAD ** -0.5),
        "rg_bx": nrm(ks[18], (DEPTH, D_RNN), 0.01),
        "rg_lambda": jnp.log(s0) - jnp.log1p(-s0),
        "w_out": nrm(ks[19], (DEPTH, D_MIX, D_MODEL), D_MIX ** -0.5 * BETA),
        "ln1_g": 1.0 + nrm(ks[21], (DEPTH, D_MODEL), 0.02),
        "ln1_b": nrm(ks[22], (DEPTH, D_MODEL), 0.02),
        "w_rg": nrm(ks[23], (DEPTH, D_MODEL, N_EXPERT_GROUPS), D_MODEL ** -0.5),
        "b_rg": nrm(ks[24], (DEPTH, N_EXPERT_GROUPS), 0.01),
        "w_re": nrm(ks[25], (DEPTH, D_MODEL, N_EXPERTS), D_MODEL ** -0.5),
        "b_re": nrm(ks[26], (DEPTH, N_EXPERTS), 0.01),
        "w_eg": nrm(ks[27], (DEPTH, N_EXPERTS, D_MODEL, D_EXPERT), D_MODEL ** -0.5),
        "w_eu": nrm(ks[28], (DEPTH, N_EXPERTS, D_MODEL, D_EXPERT), D_MODEL ** -0.5),
        "w_ed": nrm(ks[29], (DEPTH, N_EXPERTS, D_EXPERT, D_MODEL), D_EXPERT ** -0.5 * BETA),
        "ln2_g": 1.0 + nrm(ks[30], (DEPTH, D_MODEL), 0.02),
        "ln2_b": nrm(ks[31], (DEPTH, D_MODEL), 0.02),
        "w_pg": nrm(ks[32], (DEPTH, D_MODEL, D_MODEL), D_MODEL ** -0.5),
        "b_pg": nrm(ks[33], (DEPTH, D_MODEL), 0.01),
        "w_ple": nrm(ks[34], (DEPTH, D_PLE, D_MODEL), D_PLE ** -0.5 * BETA),
        "ln3_g": 1.0 + nrm(ks[35], (DEPTH, D_MODEL), 0.02),
        "ln3_b": nrm(ks[36], (DEPTH, D_MODEL), 0.02),
    }


def reference(x_prompt, x_sample, state_pool, state_conv, state_h, p_prompt, p_sample,
              ln_in_g, ln_in_b, w_in, pool_w, pool_b, pool_scale, conv_w, conv_b,
              rg_wa, rg_ba, rg_wx, rg_bx, rg_lambda, w_out, ln1_g, ln1_b,
              w_rg, b_rg, w_re, b_re, w_eg, w_eu, w_ed, ln2_g, ln2_b,
              w_pg, b_pg, w_ple, ln3_g, ln3_b):
    bp = x_prompt.shape[0]
    xp = layer_norm(x_prompt, ln_in_g, ln_in_b)
    xs = layer_norm(x_sample, ln_in_g, ln_in_b)
    zero_pool = jnp.zeros((bp, POOL_BUF, D_POOL), F32)
    zero_conv = jnp.zeros((bp, CONV_WIDTH - 1, D_RNN), F32)
    zero_h = jnp.zeros((bp, D_RNN), F32)
    pool_p, conv_p, h_p, pool_s, conv_s, h_s = [], [], [], [], [], []
    for l in range(DEPTH):
        lw = (w_in[l], pool_w[l], pool_b[l], pool_scale[l], conv_w[l], conv_b[l],
              rg_wa[l], rg_ba[l], rg_wx[l], rg_bx[l], rg_lambda[l], w_out[l], ln1_g[l], ln1_b[l],
              w_rg[l], b_rg[l], w_re[l], b_re[l], w_eg[l], w_eu[l], w_ed[l], ln2_g[l], ln2_b[l],
              w_pg[l], b_pg[l], w_ple[l], ln3_g[l], ln3_b[l])
        xp, a1, a2, a3 = decoder_layer(xp, p_prompt[l], 0, zero_pool, zero_conv, zero_h, *lw)
        xs, b1, b2, b3 = decoder_layer(xs, p_sample[l], PAST_LEN, state_pool[l], state_conv[l], state_h[l], *lw)
        pool_p.append(a1); conv_p.append(a2); h_p.append(a3)
        pool_s.append(b1); conv_s.append(b2); h_s.append(b3)
    y_prompt = xp.astype(x_prompt.dtype)
    y_sample = xs.astype(x_sample.dtype)
    return (y_prompt, y_sample, jnp.stack(pool_p), jnp.stack(conv_p), jnp.stack(h_p),
            jnp.stack(pool_s), jnp.stack(conv_s), jnp.stack(h_s))
```

```python
import functools

import jax
import jax.numpy as jnp
from jax import lax
from jax.experimental import pallas as pl
from jax.experimental.pallas import tpu as pltpu

F32 = jnp.float32
BF16 = jnp.bfloat16

D_MODEL = 1024
D_POOL = 512
D_RNN = 512
POOL_WINDOWS = (2, 4, 8, 16)
POOL_GROUP = 128
POOL_BUF = 15
N_RNN_HEADS = 8
RNN_HEAD = 64
CONV_WIDTH = 4
LRU_C = 8.0
N_EXPERT_GROUPS = 4
EXPERTS_PER_GROUP = 4
N_EXPERTS = 16
D_EXPERT = 512
D_PLE = 256
LN_EPS = 1e-5
DEPTH = 1
ALPHA = (2 * DEPTH) ** 0.25
PAST_LEN = 16384

ROUTER_ROWS = 32
POOL_HALO = 16
CONV_HALO = 8
PROMPT_TILE = 256
MOE_TILE = 512
VMEM_LIMIT = 56 * 1024 * 1024


def _layer_norm(x, g, b):
    mu = jnp.mean(x, axis=-1, keepdims=True)
    xc = x - mu
    var = jnp.mean(xc * xc, axis=-1, keepdims=True)
    return xc * lax.rsqrt(var + LN_EPS) * g + b


def _bdot(a, w):
    return jnp.dot(a.astype(BF16), w, preferred_element_type=F32)


def _gates(xc, gate_w_ref, ba, bx):
    r_parts, i_parts = [], []
    for j in range(D_RNN // 128):
        o = _bdot(xc[:, 128 * j:128 * (j + 1)], gate_w_ref[j])
        r_parts.append(o[:, :128])
        i_parts.append(o[:, 128:])
    r = jax.nn.sigmoid(jnp.concatenate(r_parts, axis=-1) + ba)
    i = jax.nn.sigmoid(jnp.concatenate(i_parts, axis=-1) + bx)
    return r, i


def _lru_terms(xc, r, i, lam, first_pos_mask):
    log_a = -LRU_C * r * jax.nn.softplus(-lam)
    a = jnp.exp(log_a)
    mult = jnp.sqrt(jnp.tanh(-log_a) * (a * a + 1.0))
    if first_pos_mask is not None:
        mult = jnp.where(first_pos_mask, 1.0, mult)
    return a, mult * i * xc


def _pool_project(d, pool_w_ref, pool_b, pool_scale):
    outs = []
    for g in range(len(POOL_WINDOWS)):
        outs.append(_bdot(d[:, 128 * g:128 * (g + 1)], pool_w_ref[g]))
    return (jnp.concatenate(outs, axis=-1) + pool_b) * pool_scale


def _route(x1, wr_hi_ref, wr_lo_ref, br):
    hi = x1.astype(BF16)
    lo = (x1 - hi.astype(F32)).astype(BF16)
    nt = (((1,), (1,)), ((), ()))
    logits = (lax.dot_general(wr_hi_ref[...], hi, nt, preferred_element_type=F32)
              + lax.dot_general(wr_hi_ref[...], lo, nt, preferred_element_type=F32)
              + lax.dot_general(wr_lo_ref[...], hi, nt, preferred_element_type=F32)) + br
    lg = [logits[j:j + 1, :] for j in range(N_EXPERT_GROUPS)]
    m = functools.reduce(jnp.maximum, lg)
    ex = [jnp.exp(v - m) for v in lg]
    den = functools.reduce(lambda p, q: p + q, ex)
    gp = [v / den for v in ex]
    g_w, g_idx = gp[0], jnp.zeros_like(gp[0], dtype=jnp.int32)
    for j in range(1, N_EXPERT_GROUPS):
        upd = gp[j] > g_w
        g_idx = jnp.where(upd, j, g_idx)
        g_w = jnp.where(upd, gp[j], g_w)
    e_in = []
    for k in range(EXPERTS_PER_GROUP):
        v = jnp.zeros_like(g_w)
        for g in range(N_EXPERT_GROUPS):
            row = N_EXPERT_GROUPS + g * EXPERTS_PER_GROUP + k
            v = v + jnp.where(g_idx == g, logits[row:row + 1, :], 0.0)
        e_in.append(v)
    m = functools.reduce(jnp.maximum, e_in)
    ex = [jnp.exp(v - m) for v in e_in]
    den = functools.reduce(lambda p, q: p + q, ex)
    ep = [v / den for v in ex]
    w0, i0 = ep[0], jnp.zeros_like(g_idx)
    for k in range(1, EXPERTS_PER_GROUP):
        upd = ep[k] > w0
        i0 = jnp.where(upd, k, i0)
        w0 = jnp.where(upd, ep[k], w0)
    w1, i1 = jnp.full_like(w0, -1.0), jnp.zeros_like(g_idx)
    for k in range(EXPERTS_PER_GROUP):
        upd = (ep[k] > w1) & (i0 != k)
        i1 = jnp.where(upd, k, i1)
        w1 = jnp.where(upd, ep[k], w1)
    wsum = w0 + w1
    c0 = g_w * (w0 / wsum)
    c1 = g_w * (w1 / wsum)
    id0 = g_idx * EXPERTS_PER_GROUP + i0
    id1 = g_idx * EXPERTS_PER_GROUP + i1
    t = x1.shape[0]
    eidx = lax.broadcasted_iota(jnp.int32, (N_EXPERTS, t), 0)
    return jnp.where(eidx == id0, c0, 0.0) + jnp.where(eidx == id1, c1, 0.0)


def _shift_rows(x, s):
    return pltpu.roll(x, s, 0)


def _prompt_kernel(x_ref, ln_in_g, ln_in_b, w_in, pool_w, pool_b, pool_scale, conv_w, conv_b,
                   gate_w, ba, bx, lam, w_out, ln1_g, ln1_b, wr_hi, wr_lo, br,
                   x1_ref, comb_ref, pool_new_ref, conv_new_ref, h_new_ref,
                   pool_halo, conv_halo, h_carry):
    t_idx = pl.program_id(1)
    tt = x_ref.shape[0]

    @pl.when(t_idx == 0)
    def _():
        pool_halo[...] = jnp.zeros_like(pool_halo)
        conv_halo[...] = jnp.zeros_like(conv_halo)
        h_carry[...] = jnp.zeros_like(h_carry)

    xn = _layer_norm(x_ref[...], ln_in_g[...], ln_in_b[...])
    proj = _bdot(xn, w_in[...])
    u = proj[:, :D_POOL]
    xr = proj[:, D_POOL:D_POOL + D_RNN]
    gate = proj[:, D_POOL + D_RNN:]

    row = lax.broadcasted_iota(jnp.int32, (tt, 1), 0)
    pos = t_idx * tt + row

    e = jnp.concatenate([pool_halo[...], u], axis=0)
    s = e + _shift_rows(e, 1)
    sums = [s[:, :POOL_GROUP]]
    s = s[:, POOL_GROUP:]
    for step in (2, 4, 8):
        s = s + _shift_rows(s, step)
        sums.append(s[:, :POOL_GROUP])
        s = s[:, POOL_GROUP:]
    d_parts = []
    for g, win in enumerate(POOL_WINDOWS):
        cnt = jnp.minimum(pos + 1, win).astype(F32)
        d_parts.append(sums[g][POOL_HALO:, :] / cnt - u[:, POOL_GROUP * g:POOL_GROUP * (g + 1)])
    pool_out = _pool_project(jnp.concatenate(d_parts, axis=-1), pool_w, pool_b[...], pool_scale[...])
    pool_halo[...] = u[tt - POOL_HALO:, :]

    ec = jnp.concatenate([conv_halo[...], xr], axis=0)
    xc = conv_b[...] + _shift_rows(ec, 3) * conv_w[0:1, :]
    xc = xc + _shift_rows(ec, 2) * conv_w[1:2, :]
    xc = xc + _shift_rows(ec, 1) * conv_w[2:3, :]
    xc = (xc + ec * conv_w[3:4, :])[CONV_HALO:, :]
    conv_halo[...] = xr[tt - CONV_HALO:, :]

    r, i = _gates(xc, gate_w, ba[...], bx[...])
    a, b = _lru_terms(xc, r, i, lam[...], pos == 0)
    b = b + jnp.where(row == 0, a * h_carry[...], 0.0)
    step = 1
    while step < tt:
        keep = row >= step
        a_prev = jnp.where(keep, _shift_rows(a, step), 1.0)
        b_prev = jnp.where(keep, _shift_rows(b, step), 0.0)
        b = a * b_prev + b
        a = a * a_prev
        step *= 2
    h = b
    h_carry[...] = h[tt - 1:tt, :]
    rnn_out = h * jax.nn.gelu(gate)

    mix = _bdot(jnp.concatenate([pool_out, rnn_out], axis=-1), w_out[...])
    x1 = _layer_norm(ALPHA * xn + mix, ln1_g[...], ln1_b[...])
    x1_ref[...] = x1
    comb_ref[...] = _route(x1, wr_hi, wr_lo, br[...])

    @pl.when(t_idx == pl.num_programs(1) - 1)
    def _():
        pool_new_ref[...] = u[tt - POOL_HALO:, :]
        conv_new_ref[...] = xr[tt - CONV_HALO:, :]
        h_new_ref[...] = jnp.broadcast_to(h[tt - 1:tt, :], h_new_ref.shape)


def _sample_kernel(x_ref, pool_buf_ref, conv_buf_ref, h0_ref,
                   ln_in_g, ln_in_b, w_in, pool_w, pool_b, pool_scale, conv_w, conv_b,
                   gate_w, ba, bx, lam, w_out, ln1_g, ln1_b, wr_hi, wr_lo, br,
                   x1_ref, comb_ref, pool_new_ref, conv_new_ref, h_new_ref):
    nb = h0_ref.shape[0]
    t_len = x_ref.shape[0] // nb
    xn = _layer_norm(x_ref[...], ln_in_g[...], ln_in_b[...])
    proj = _bdot(xn, w_in[...])
    u = proj[:, :D_POOL]
    xr = proj[:, D_POOL:D_POOL + D_RNN]
    gate = proj[:, D_POOL + D_RNN:]

    f = [pool_buf_ref[k] for k in range(POOL_BUF)] + [u[nb * t:nb * (t + 1), :] for t in range(t_len)]
    n_rows = len(f)
    s = [f[0]] + [f[k] + f[k - 1] for k in range(1, n_rows)]
    sums = [[v[:, :POOL_GROUP] for v in s]]
    s = [v[:, POOL_GROUP:] for v in s]
    for step in (2, 4, 8):
        s = [s[k] + s[k - step] if k >= step else s[k] for k in range(n_rows)]
        sums.append([v[:, :POOL_GROUP] for v in s])
        s = [v[:, POOL_GROUP:] for v in s]
    d_rows = []
    for t in range(t_len):
        parts = []
        for g, win in enumerate(POOL_WINDOWS):
            cnt = float(min(PAST_LEN + t + 1, win))
            parts.append(sums[g][POOL_BUF + t] / cnt - f[POOL_BUF + t][:, POOL_GROUP * g:POOL_GROUP * (g + 1)])
        d_rows.append(jnp.concatenate(parts, axis=-1))
    pool_out = _pool_project(jnp.concatenate(d_rows, axis=0), pool_w, pool_b[...], pool_scale[...])
    for k in range(POOL_BUF):
        pool_new_ref[k] = f[n_rows - POOL_BUF + k]

    gbuf = [conv_buf_ref[k] for k in range(CONV_WIDTH - 1)] + [xr[nb * t:nb * (t + 1), :] for t in range(t_len)]
    xc_rows = []
    for t in range(t_len):
        v = conv_b[...] + gbuf[t] * conv_w[0:1, :]
        for k in range(1, CONV_WIDTH):
            v = v + gbuf[t + k] * conv_w[k:k + 1, :]
        xc_rows.append(v)
    xc = jnp.concatenate(xc_rows, axis=0)
    for k in range(CONV_WIDTH - 1):
        conv_new_ref[k] = gbuf[len(gbuf) - (CONV_WIDTH - 1) + k]

    r, i = _gates(xc, gate_w, ba[...], bx[...])
    a, b = _lru_terms(xc, r, i, lam[...], None)
    h = h0_ref[...]
    h_rows = []
    for t in range(t_len):
        h = a[nb * t:nb * (t + 1), :] * h + b[nb * t:nb * (t + 1), :]
        h_rows.append(h)
    h_new_ref[...] = h
    rnn_out = jnp.concatenate(h_rows, axis=0) * jax.nn.gelu(gate)

    mix = _bdot(jnp.concatenate([pool_out, rnn_out], axis=-1), w_out[...])
    x1 = _layer_norm(ALPHA * xn + mix, ln1_g[...], ln1_b[...])
    x1_ref[...] = x1
    comb_ref[...] = _route(x1, wr_hi, wr_lo, br[...])


def _moe_kernel(x1_ref, comb_ref, pe_ref, w_eg, w_eu, w_ed, ln2_g, ln2_b, w_pg, b_pg, w_ple, ln3_g, ln3_b,
                out_ref, acc, xb):
    e = pl.program_id(1)

    @pl.when(e == 0)
    def _():
        acc[...] = jnp.zeros_like(acc)
        xb[...] = x1_ref[...].astype(BF16)

    x = xb[...]
    hid = jax.nn.silu(jnp.dot(x, w_eg[...], preferred_element_type=F32)) * jnp.dot(
        x, w_eu[...], preferred_element_type=F32)
    y = _bdot(hid, w_ed[...])
    lane = lax.broadcasted_iota(jnp.int32, comb_ref.shape, 1)
    c = jnp.sum(jnp.where(lane == e, comb_ref[...], 0.0), axis=1, keepdims=True)
    acc[...] += c * y

    @pl.when(e == pl.num_programs(1) - 1)
    def _():
        x2 = _layer_norm(ALPHA * x1_ref[...] + acc[...], ln2_g[...], ln2_b[...])
        gate_p = jax.nn.sigmoid(_bdot(x2, w_pg[...]) + b_pg[...])
        ple = _bdot(pe_ref[...], w_ple[...]) * gate_p
        out_ref[...] = _layer_norm(ALPHA * x2 + ple, ln3_g[...], ln3_b[...])


def _full(shape):
    nd = len(shape)
    return pl.BlockSpec(shape, lambda *_: (0,) * nd)


def _row(v):
    return v.reshape(1, -1).astype(F32)


def kernel(x_prompt, x_sample, state_pool, state_conv, state_h, p_prompt, p_sample, ln_in_g, ln_in_b, w_in,
           pool_w, pool_b, pool_scale, conv_w, conv_b, rg_wa, rg_ba, rg_wx, rg_bx, rg_lambda, w_out, ln1_g,
           ln1_b, w_rg, b_rg, w_re, b_re, w_eg, w_eu, w_ed, ln2_g, ln2_b, w_pg, b_pg, w_ple, ln3_g, ln3_b):
    bp, seq, _ = x_prompt.shape
    bs, dseq, _ = x_sample.shape
    n_p, n_s = bp * seq, bs * dseq
    n_tok = n_p + n_s
    assert seq % PROMPT_TILE == 0 and n_tok % MOE_TILE == 0

    w_in_b = w_in[0].astype(BF16)
    w_out_b = w_out[0].astype(BF16)
    pool_w_b = pool_w[0].astype(BF16)
    zero = jnp.zeros((RNN_HEAD, RNN_HEAD), F32)

    def pair_block(w, j):
        top = jnp.concatenate([w[2 * j], zero], axis=1)
        bot = jnp.concatenate([zero, w[2 * j + 1]], axis=1)
        return jnp.concatenate([top, bot], axis=0)

    gate_w = jnp.stack([jnp.concatenate([pair_block(rg_wa[0], j), pair_block(rg_wx[0], j)], axis=1)
                        for j in range(N_RNN_HEADS // 2)]).astype(BF16)
    wr = jnp.concatenate([w_rg[0].T, w_re[0].T,
                          jnp.zeros((ROUTER_ROWS - N_EXPERT_GROUPS - N_EXPERTS, D_MODEL), F32)], axis=0)
    wr_hi = wr.astype(BF16)
    wr_lo = (wr - wr_hi.astype(F32)).astype(BF16)
    br = jnp.concatenate([b_rg[0], b_re[0],
                          jnp.zeros((ROUTER_ROWS - N_EXPERT_GROUPS - N_EXPERTS,), F32)]).reshape(ROUTER_ROWS, 1)

    mixer_params = (_row(ln_in_g), _row(ln_in_b), w_in_b, pool_w_b, _row(pool_b[0]), _row(pool_scale[0]),
                    conv_w[0], _row(conv_b[0]), gate_w, _row(rg_ba[0]), _row(rg_bx[0]), _row(rg_lambda[0]),
                    w_out_b, _row(ln1_g[0]), _row(ln1_b[0]), wr_hi, wr_lo, br)
    mixer_specs = [_full(p.shape) for p in mixer_params]

    n_t = seq // PROMPT_TILE
    x1_p, comb_p, pool_p, conv_p, h_p = pl.pallas_call(
        _prompt_kernel,
        grid=(bp, n_t),
        in_specs=[pl.BlockSpec((None, PROMPT_TILE, D_MODEL), lambda b, t: (b, t, 0))] + mixer_specs,
        out_specs=[
            pl.BlockSpec((PROMPT_TILE, D_MODEL), lambda b, t: (b * n_t + t, 0)),
            pl.BlockSpec((N_EXPERTS, PROMPT_TILE), lambda b, t: (0, b * n_t + t)),
            pl.BlockSpec((None, POOL_HALO, D_POOL), lambda b, t: (b, 0, 0)),
            pl.BlockSpec((None, CONV_HALO, D_RNN), lambda b, t: (b, 0, 0)),
            pl.BlockSpec((None, 8, D_RNN), lambda b, t: (b, 0, 0)),
        ],
        out_shape=[
            jax.ShapeDtypeStruct((n_p, D_MODEL), F32),
            jax.ShapeDtypeStruct((N_EXPERTS, n_p), F32),
            jax.ShapeDtypeStruct((bp, POOL_HALO, D_POOL), F32),
            jax.ShapeDtypeStruct((bp, CONV_HALO, D_RNN), F32),
            jax.ShapeDtypeStruct((bp, 8, D_RNN), F32),
        ],
        scratch_shapes=[pltpu.VMEM((POOL_HALO, D_POOL), F32), pltpu.VMEM((CONV_HALO, D_RNN), F32),
                        pltpu.VMEM((1, D_RNN), F32)],
        compiler_params=pltpu.CompilerParams(dimension_semantics=("arbitrary", "arbitrary"),
                                             vmem_limit_bytes=VMEM_LIMIT),
        name="prompt_mixers",
    )(x_prompt, *mixer_params)

    xs_tm = x_sample.transpose(1, 0, 2).reshape(n_s, D_MODEL)
    pool_buf_tm = state_pool[0].transpose(1, 0, 2)
    conv_buf_tm = state_conv[0].transpose(1, 0, 2)
    sample_in = (xs_tm, pool_buf_tm, conv_buf_tm, state_h[0])
    x1_s, comb_s, pool_s, conv_s, h_s = pl.pallas_call(
        _sample_kernel,
        grid=(1,),
        in_specs=[_full(p.shape) for p in sample_in] + mixer_specs,
        out_specs=[_full((n_s, D_MODEL)), _full((N_EXPERTS, n_s)), _full((POOL_BUF, bs, D_POOL)),
                   _full((CONV_WIDTH - 1, bs, D_RNN)), _full((bs, D_RNN))],
        out_shape=[
            jax.ShapeDtypeStruct((n_s, D_MODEL), F32),
            jax.ShapeDtypeStruct((N_EXPERTS, n_s), F32),
            jax.ShapeDtypeStruct((POOL_BUF, bs, D_POOL), F32),
            jax.ShapeDtypeStruct((CONV_WIDTH - 1, bs, D_RNN), F32),
            jax.ShapeDtypeStruct((bs, D_RNN), F32),
        ],
        compiler_params=pltpu.CompilerParams(dimension_semantics=("arbitrary",), vmem_limit_bytes=VMEM_LIMIT),
        name="sample_mixers",
    )(*sample_in, *mixer_params)

    x1 = jnp.concatenate([x1_p, x1_s], axis=0)
    comb = jnp.concatenate([comb_p, comb_s], axis=1).T
    pe = jnp.concatenate([p_prompt[0].reshape(n_p, D_PLE),
                          p_sample[0].transpose(1, 0, 2).reshape(n_s, D_PLE)], axis=0)
    moe_params = (_row(ln2_g[0]), _row(ln2_b[0]), w_pg[0].astype(BF16), _row(b_pg[0]), w_ple[0].astype(BF16),
                  _row(ln3_g[0]), _row(ln3_b[0]))
    y = pl.pallas_call(
        _moe_kernel,
        grid=(n_tok // MOE_TILE, N_EXPERTS),
        in_specs=[
            pl.BlockSpec((MOE_TILE, D_MODEL), lambda i, e: (i, 0)),
            pl.BlockSpec((MOE_TILE, N_EXPERTS), lambda i, e: (i, 0)),
            pl.BlockSpec((MOE_TILE, D_PLE), lambda i, e: (i, 0)),
            pl.BlockSpec((None, D_MODEL, D_EXPERT), lambda i, e: (e, 0, 0)),
            pl.BlockSpec((None, D_MODEL, D_EXPERT), lambda i, e: (e, 0, 0)),
            pl.BlockSpec((None, D_EXPERT, D_MODEL), lambda i, e: (e, 0, 0)),
        ] + [_full(p.shape) for p in moe_params],
        out_specs=pl.BlockSpec((MOE_TILE, D_MODEL), lambda i, e: (i, 0)),
        out_shape=jax.ShapeDtypeStruct((n_tok, D_MODEL), F32),
        scratch_shapes=[pltpu.VMEM((MOE_TILE, D_MODEL), F32), pltpu.VMEM((MOE_TILE, D_MODEL), BF16)],
        compiler_params=pltpu.CompilerParams(dimension_semantics=("arbitrary", "arbitrary"),
                                             vmem_limit_bytes=VMEM_LIMIT),
        name="moe_embed",
    )(x1, comb, pe, w_eg[0].astype(BF16), w_eu[0].astype(BF16), w_ed[0].astype(BF16), *moe_params)

    y_prompt = y[:n_p].reshape(bp, seq, D_MODEL)
    y_sample = y[n_p:].reshape(dseq, bs, D_MODEL).transpose(1, 0, 2)
    return (y_prompt, y_sample,
            pool_p[None, :, POOL_HALO - POOL_BUF:, :], conv_p[None, :, CONV_HALO - (CONV_WIDTH - 1):, :],
            h_p[None, :, 0, :],
            pool_s.transpose(1, 0, 2)[None], conv_s.transpose(1, 0, 2)[None], h_s[None])
```

```python
import functools

import jax
import jax.numpy as jnp
from jax import lax
from jax.experimental import pallas as pl
from jax.experimental.pallas import tpu as pltpu

F32 = jnp.float32
BF16 = jnp.bfloat16
I32 = jnp.int32

D_MODEL = 1024
D_POOL = 512
D_RNN = 512
POOL_WINDOWS = (2, 4, 8, 16)
POOL_GROUP = 128
POOL_BUF = 15
N_RNN_HEADS = 8
RNN_HEAD = 64
CONV_WIDTH = 4
LRU_C = 8.0
N_EXPERT_GROUPS = 4
EXPERTS_PER_GROUP = 4
N_EXPERTS = 16
D_EXPERT = 512
D_PLE = 256
LN_EPS = 1e-5
DEPTH = 1
ALPHA = (2 * DEPTH) ** 0.25
PAST_LEN = 16384

LANES = 128
SUBLANES = 8
ROUTER_ROWS = 32
PAIRS = ((0, 1), (0, 2), (0, 3), (1, 2), (1, 3), (2, 3))
N_CLASSES = N_EXPERT_GROUPS * len(PAIRS)
CLASS_ROWS = 32
POOL_HALO = 16
CONV_HALO = 8
PROMPT_TILE = 256
MOE_TILE = 256
D_PAY = D_MODEL + LANES + D_PLE
VMEM_LIMIT = 56 * 1024 * 1024


def _layer_norm(x, g, b):
    mu = jnp.mean(x, axis=-1, keepdims=True)
    xc = x - mu
    var = jnp.mean(xc * xc, axis=-1, keepdims=True)
    return xc * lax.rsqrt(var + LN_EPS) * g + b


def _bdot(a, w):
    return jnp.dot(a.astype(BF16), w, preferred_element_type=F32)


def _gates(xc, gate_w_ref, ba, bx):
    r_parts, i_parts = [], []
    for j in range(D_RNN // LANES):
        o = _bdot(xc[:, LANES * j:LANES * (j + 1)], gate_w_ref[j])
        r_parts.append(o[:, :LANES])
        i_parts.append(o[:, LANES:])
    r = jax.nn.sigmoid(jnp.concatenate(r_parts, axis=-1) + ba)
    i = jax.nn.sigmoid(jnp.concatenate(i_parts, axis=-1) + bx)
    return r, i


def _lru_terms(xc, r, i, lam, first_pos_mask):
    log_a = -LRU_C * r * jax.nn.softplus(-lam)
    a = jnp.exp(log_a)
    mult = jnp.sqrt(jnp.tanh(-log_a) * (a * a + 1.0))
    if first_pos_mask is not None:
        mult = jnp.where(first_pos_mask, 1.0, mult)
    return a, mult * i * xc


def _pool_project(d, pool_w_ref, pool_b, pool_scale):
    outs = []
    for g in range(len(POOL_WINDOWS)):
        outs.append(_bdot(d[:, POOL_GROUP * g:POOL_GROUP * (g + 1)], pool_w_ref[g]))
    return (jnp.concatenate(outs, axis=-1) + pool_b) * pool_scale


def _route(x1, wr_hi_ref, wr_lo_ref, br):
    hi = x1.astype(BF16)
    lo = (x1 - hi.astype(F32)).astype(BF16)
    nt = (((1,), (1,)), ((), ()))
    logits = (lax.dot_general(wr_hi_ref[...], hi, nt, preferred_element_type=F32)
              + lax.dot_general(wr_hi_ref[...], lo, nt, preferred_element_type=F32)
              + lax.dot_general(wr_lo_ref[...], hi, nt, preferred_element_type=F32)) + br
    lg = [logits[j:j + 1, :] for j in range(N_EXPERT_GROUPS)]
    m = functools.reduce(jnp.maximum, lg)
    ex = [jnp.exp(v - m) for v in lg]
    den = functools.reduce(lambda p, q: p + q, ex)
    gp = [v / den for v in ex]
    g_w, g_idx = gp[0], jnp.zeros_like(gp[0], dtype=I32)
    for j in range(1, N_EXPERT_GROUPS):
        upd = gp[j] > g_w
        g_idx = jnp.where(upd, j, g_idx)
        g_w = jnp.where(upd, gp[j], g_w)
    e_in = []
    for k in range(EXPERTS_PER_GROUP):
        v = jnp.zeros_like(g_w)
        for g in range(N_EXPERT_GROUPS):
            row = N_EXPERT_GROUPS + g * EXPERTS_PER_GROUP + k
            v = v + jnp.where(g_idx == g, logits[row:row + 1, :], 0.0)
        e_in.append(v)
    m = functools.reduce(jnp.maximum, e_in)
    ex = [jnp.exp(v - m) for v in e_in]
    den = functools.reduce(lambda p, q: p + q, ex)
    ep = [v / den for v in ex]
    w0, i0 = ep[0], jnp.zeros_like(g_idx)
    for k in range(1, EXPERTS_PER_GROUP):
        upd = ep[k] > w0
        i0 = jnp.where(upd, k, i0)
        w0 = jnp.where(upd, ep[k], w0)
    w1, i1 = jnp.full_like(w0, -1.0), jnp.zeros_like(g_idx)
    for k in range(EXPERTS_PER_GROUP):
        upd = (ep[k] > w1) & (i0 != k)
        i1 = jnp.where(upd, k, i1)
        w1 = jnp.where(upd, ep[k], w1)
    wsum = w0 + w1
    c0 = g_w * (w0 / wsum)
    c1 = g_w * (w1 / wsum)
    first_is_lo = i0 < i1
    e_lo = jnp.minimum(i0, i1)
    e_hi = jnp.maximum(i0, i1)
    pair = jnp.zeros_like(g_idx)
    for p, (a, b) in enumerate(PAIRS):
        pair = jnp.where((e_lo == a) & (e_hi == b), p, pair)
    cls = g_idx * len(PAIRS) + pair
    return cls, jnp.where(first_is_lo, c0, c1), jnp.where(first_is_lo, c1, c0)


def _rank_in_class(cls, cnt_ref):
    t = cls.shape[1]
    onehot = (lax.broadcasted_iota(I32, (CLASS_ROWS, t), 0) == cls).astype(F32)
    before = (lax.broadcasted_iota(I32, (t, t), 0) < lax.broadcasted_iota(I32, (t, t), 1)).astype(BF16)
    prefix = jnp.dot(onehot.astype(BF16), before, preferred_element_type=F32)
    rank = jnp.sum(onehot * (prefix + cnt_ref[...]), axis=0, keepdims=True)
    cnt_ref[...] = cnt_ref[...] + jnp.sum(onehot, axis=1, keepdims=True)
    return rank


def _write_token_rows(pay_ref, route_ref, x1, pe, cls, rank, c_lo, c_hi):
    t = x1.shape[0]
    meta_t = jnp.concatenate([c_lo, c_hi, jnp.zeros((LANES - 2, t), F32)], axis=0)
    pay_ref[:, :D_MODEL] = x1
    pay_ref[:, D_MODEL:D_MODEL + LANES] = meta_t.T
    pay_ref[:, D_MODEL + LANES:] = pe
    route_ref[...] = jnp.concatenate([cls.astype(F32), rank, jnp.zeros((SUBLANES - 2, t), F32)], axis=0)


def _shift_rows(x, s):
    return pltpu.roll(x, s, 0)


def _prompt_kernel(x_ref, pe_ref, ln_in_g, ln_in_b, w_in, pool_w, pool_b, pool_scale, conv_w, conv_b,
                   gate_w, ba, bx, lam, w_out, ln1_g, ln1_b, wr_hi, wr_lo, br,
                   pay_ref, route_ref, cnt_out_ref, pool_new_ref, conv_new_ref, h_new_ref,
                   pool_halo, conv_halo, h_carry, cnt):
    t_idx = pl.program_id(1)
    tt = x_ref.shape[0]

    @pl.when((pl.program_id(0) == 0) & (t_idx == 0))
    def _():
        cnt[...] = jnp.zeros_like(cnt)

    @pl.when(t_idx == 0)
    def _():
        pool_halo[...] = jnp.zeros_like(pool_halo)
        conv_halo[...] = jnp.zeros_like(conv_halo)
        h_carry[...] = jnp.zeros_like(h_carry)

    xn = _layer_norm(x_ref[...], ln_in_g[...], ln_in_b[...])
    proj = _bdot(xn, w_in[...])
    u = proj[:, :D_POOL]
    xr = proj[:, D_POOL:D_POOL + D_RNN]
    gate = proj[:, D_POOL + D_RNN:]

    row = lax.broadcasted_iota(I32, (tt, 1), 0)
    pos = t_idx * tt + row

    e = jnp.concatenate([pool_halo[...], u], axis=0)
    s = e + _shift_rows(e, 1)
    sums = [s[:, :POOL_GROUP]]
    s = s[:, POOL_GROUP:]
    for step in (2, 4, 8):
        s = s + _shift_rows(s, step)
        sums.append(s[:, :POOL_GROUP])
        s = s[:, POOL_GROUP:]
    d_parts = []
    for g, win in enumerate(POOL_WINDOWS):
        cnt_w = jnp.minimum(pos + 1, win).astype(F32)
        d_parts.append(sums[g][POOL_HALO:, :] / cnt_w - u[:, POOL_GROUP * g:POOL_GROUP * (g + 1)])
    pool_out = _pool_project(jnp.concatenate(d_parts, axis=-1), pool_w, pool_b[...], pool_scale[...])
    pool_halo[...] = u[tt - POOL_HALO:, :]

    ec = jnp.concatenate([conv_halo[...], xr], axis=0)
    xc = conv_b[...] + _shift_rows(ec, 3) * conv_w[0:1, :]
    xc = xc + _shift_rows(ec, 2) * conv_w[1:2, :]
    xc = xc + _shift_rows(ec, 1) * conv_w[2:3, :]
    xc = (xc + ec * conv_w[3:4, :])[CONV_HALO:, :]
    conv_halo[...] = xr[tt - CONV_HALO:, :]

    r, i = _gates(xc, gate_w, ba[...], bx[...])
    a, b = _lru_terms(xc, r, i, lam[...], pos == 0)
    b = b + jnp.where(row == 0, a * h_carry[...], 0.0)
    step = 1
    while step < tt:
        keep = row >= step
        a_prev = jnp.where(keep, _shift_rows(a, step), 1.0)
        b_prev = jnp.where(keep, _shift_rows(b, step), 0.0)
        b = a * b_prev + b
        a = a * a_prev
        step *= 2
    h = b
    h_carry[...] = h[tt - 1:tt, :]
    rnn_out = h * jax.nn.gelu(gate)

    mix = _bdot(jnp.concatenate([pool_out, rnn_out], axis=-1), w_out[...])
    x1 = _layer_norm(ALPHA * xn + mix, ln1_g[...], ln1_b[...])
    cls, c_lo, c_hi = _route(x1, wr_hi, wr_lo, br[...])
    rank = _rank_in_class(cls, cnt)
    _write_token_rows(pay_ref, route_ref, x1, pe_ref[...], cls, rank, c_lo, c_hi)
    cnt_out_ref[...] = cnt[...]

    @pl.when(t_idx == pl.num_programs(1) - 1)
    def _():
        pool_new_ref[...] = u[tt - POOL_HALO:, :]
        conv_new_ref[...] = xr[tt - CONV_HALO:, :]
        h_new_ref[...] = jnp.broadcast_to(h[tt - 1:tt, :], h_new_ref.shape)


def _sample_kernel(x_ref, pe_ref, pool_buf_ref, conv_buf_ref, h0_ref, cnt_in_ref,
                   ln_in_g, ln_in_b, w_in, pool_w, pool_b, pool_scale, conv_w, conv_b,
                   gate_w, ba, bx, lam, w_out, ln1_g, ln1_b, wr_hi, wr_lo, br,
                   pay_ref, route_ref, cnt_out_ref, pool_new_ref, conv_new_ref, h_new_ref):
    nb = h0_ref.shape[0]
    t_len = x_ref.shape[0] // nb
    xn = _layer_norm(x_ref[...], ln_in_g[...], ln_in_b[...])
    proj = _bdot(xn, w_in[...])
    u = proj[:, :D_POOL]
    xr = proj[:, D_POOL:D_POOL + D_RNN]
    gate = proj[:, D_POOL + D_RNN:]

    f = [pool_buf_ref[k] for k in range(POOL_BUF)] + [u[nb * t:nb * (t + 1), :] for t in range(t_len)]
    n_rows = len(f)
    s = [f[0]] + [f[k] + f[k - 1] for k in range(1, n_rows)]
    sums = [[v[:, :POOL_GROUP] for v in s]]
    s = [v[:, POOL_GROUP:] for v in s]
    for step in (2, 4, 8):
        s = [s[k] + s[k - step] if k >= step else s[k] for k in range(n_rows)]
        sums.append([v[:, :POOL_GROUP] for v in s])
        s = [v[:, POOL_GROUP:] for v in s]
    d_rows = []
    for t in range(t_len):
        parts = []
        for g, win in enumerate(POOL_WINDOWS):
            cnt_w = float(min(PAST_LEN + t + 1, win))
            parts.append(sums[g][POOL_BUF + t] / cnt_w - f[POOL_BUF + t][:, POOL_GROUP * g:POOL_GROUP * (g + 1)])
        d_rows.append(jnp.concatenate(parts, axis=-1))
    pool_out = _pool_project(jnp.concatenate(d_rows, axis=0), pool_w, pool_b[...], pool_scale[...])
    for k in range(POOL_BUF):
        pool_new_ref[k] = f[n_rows - POOL_BUF + k]

    gbuf = [conv_buf_ref[k] for k in range(CONV_WIDTH - 1)] + [xr[nb * t:nb * (t + 1), :] for t in range(t_len)]
    xc_rows = []
    for t in range(t_len):
        v = conv_b[...] + gbuf[t] * conv_w[0:1, :]
        for k in range(1, CONV_WIDTH):
            v = v + gbuf[t + k] * conv_w[k:k + 1, :]
        xc_rows.append(v)
    xc = jnp.concatenate(xc_rows, axis=0)
    for k in range(CONV_WIDTH - 1):
        conv_new_ref[k] = gbuf[len(gbuf) - (CONV_WIDTH - 1) + k]

    r, i = _gates(xc, gate_w, ba[...], bx[...])
    a, b = _lru_terms(xc, r, i, lam[...], None)
    h = h0_ref[...]
    h_rows = []
    for t in range(t_len):
        h = a[nb * t:nb * (t + 1), :] * h + b[nb * t:nb * (t + 1), :]
        h_rows.append(h)
    h_new_ref[...] = h
    rnn_out = jnp.concatenate(h_rows, axis=0) * jax.nn.gelu(gate)

    mix = _bdot(jnp.concatenate([pool_out, rnn_out], axis=-1), w_out[...])
    x1 = _layer_norm(ALPHA * xn + mix, ln1_g[...], ln1_b[...])
    cls, c_lo, c_hi = _route(x1, wr_hi, wr_lo, br[...])
    cnt_out_ref[...] = cnt_in_ref[...]
    rank = _rank_in_class(cls, cnt_out_ref)
    _write_token_rows(pay_ref, route_ref, x1, pe_ref[...], cls, rank, c_lo, c_hi)


def _moe_kernel(dest_ref, nv_ref, nvp_ref, ea_ref, eb_ref,
                payp_hbm, pays_hbm, wg_a, wu_a, wd_a, wg_b, wu_b, wd_b,
                ln2_g, ln2_b, w_pg, b_pg, w_ple, ln3_g, ln3_b,
                yp_hbm, ys_hbm, trash_hbm,
                src, gbuf, obuf, gsem, ssem, *, sample_batch, sample_seq):
    del ea_ref, eb_ref
    t = pl.program_id(0)
    n_tiles = pl.num_programs(0)
    tm = gbuf.shape[1]
    slot = t % 2
    n_tok = dest_ref.shape[0]
    n_prompt = payp_hbm.shape[0]
    log2_batch = sample_batch.bit_length() - 1

    def rows8(tile):
        n = nv_ref[tile]
        return pl.multiple_of((n + (SUBLANES - 1)) & -SUBLANES, SUBLANES)

    def issue_gather(tile, slot_):
        base = tile * tm

        def dst(j):
            return gbuf.at[slot_, pl.ds(j, 1), :]

        def from_prompt(j, c):
            pltpu.make_async_copy(payp_hbm.at[pl.ds(src[base + j], 1), :], dst(j), gsem.at[slot_]).start()
            return c
        lax.fori_loop(0, nvp_ref[tile], from_prompt, 0)

        def from_sample(j, c):
            pltpu.make_async_copy(pays_hbm.at[pl.ds(src[base + j] - n_prompt, 1), :], dst(j), gsem.at[slot_]).start()
            return c
        lax.fori_loop(nvp_ref[tile], nv_ref[tile], from_sample, 0)

        def rounding(j, c):
            pltpu.make_async_copy(payp_hbm.at[pl.ds(0, 1), :], dst(j), gsem.at[slot_]).start()
            return c
        lax.fori_loop(nv_ref[tile], rows8(tile), rounding, 0)

    def wait_gather(tile, slot_):
        n = rows8(tile)

        @pl.when(n > 0)
        def _():
            pltpu.make_async_copy(payp_hbm.at[pl.ds(0, n), :], gbuf.at[slot_, pl.ds(0, n), :], gsem.at[slot_]).wait()

    def wait_scatter(tile, slot_):
        n = rows8(tile)

        @pl.when(n > 0)
        def _():
            pltpu.make_async_copy(obuf.at[slot_, pl.ds(0, n), :], yp_hbm.at[pl.ds(0, n), :], ssem.at[slot_]).wait()

    @pl.when(t == 0)
    def _():
        gbuf[...] = jnp.zeros_like(gbuf)
        init = pltpu.make_async_copy(gbuf.at[0, pl.ds(0, 2 * SUBLANES), pl.ds(0, D_MODEL)], trash_hbm, ssem.at[0])
        init.start()
        init.wait()

        def fill(i, c):
            src[dest_ref[i]] = i
            return c
        lax.fori_loop(0, n_tok, fill, 0)
        issue_gather(0, 0)

    @pl.when(t + 1 < n_tiles)
    def _():
        issue_gather(t + 1, 1 - slot)

    wait_gather(t, slot)

    @pl.when(t >= 2)
    def _():
        wait_scatter(t - 2, slot)

    nv = nv_ref[t]

    @pl.when(nv > 0)
    def _():
        xp = gbuf[slot]
        x1 = xp[:, :D_MODEL]
        meta = xp[:, D_MODEL:D_MODEL + LANES]
        pe = xp[:, D_MODEL + LANES:]
        xb = x1.astype(BF16)

        def expert(wg, wu, wd):
            hid = jax.nn.silu(jnp.dot(xb, wg[...], preferred_element_type=F32)) * jnp.dot(
                xb, wu[...], preferred_element_type=F32)
            return _bdot(hid, wd[...])

        moe = meta[:, 0:1] * expert(wg_a, wu_a, wd_a)
        moe = moe + meta[:, 1:2] * expert(wg_b, wu_b, wd_b)
        x2 = _layer_norm(ALPHA * x1 + moe, ln2_g[...], ln2_b[...])
        gate_p = jax.nn.sigmoid(_bdot(x2, w_pg[...]) + b_pg[...])
        ple = _bdot(pe, w_ple[...]) * gate_p
        obuf[slot] = _layer_norm(ALPHA * x2 + ple, ln3_g[...], ln3_b[...])

    base = t * tm
    nvp = nvp_ref[t]

    def out_row(j):
        return obuf.at[slot, pl.ds(j, 1), :]

    def to_prompt(j, c):
        pltpu.make_async_copy(out_row(j), yp_hbm.at[pl.ds(src[base + j], 1), :], ssem.at[slot]).start()
        return c
    lax.fori_loop(0, nvp, to_prompt, 0)

    def to_sample(j, c):
        s = src[base + j] - n_prompt
        r = (s & (sample_batch - 1)) * sample_seq + lax.shift_right_logical(s, log2_batch)
        pltpu.make_async_copy(out_row(j), ys_hbm.at[pl.ds(r, 1), :], ssem.at[slot]).start()
        return c
    lax.fori_loop(nvp, nv, to_sample, 0)

    def to_trash(j, c):
        pltpu.make_async_copy(out_row(j), trash_hbm.at[pl.ds(slot * SUBLANES + j - nv, 1), :], ssem.at[slot]).start()
        return c
    lax.fori_loop(nv, rows8(t), to_trash, 0)

    @pl.when(t == n_tiles - 1)
    def _():
        @pl.when(t >= 1)
        def _():
            wait_scatter(t - 1, 1 - slot)
        wait_scatter(t, slot)


def _full(shape):
    nd = len(shape)
    return pl.BlockSpec(shape, lambda *_: (0,) * nd)


def _row(v):
    return v.reshape(1, -1).astype(F32)


def kernel(x_prompt, x_sample, state_pool, state_conv, state_h, p_prompt, p_sample, ln_in_g, ln_in_b, w_in,
           pool_w, pool_b, pool_scale, conv_w, conv_b, rg_wa, rg_ba, rg_wx, rg_bx, rg_lambda, w_out, ln1_g,
           ln1_b, w_rg, b_rg, w_re, b_re, w_eg, w_eu, w_ed, ln2_g, ln2_b, w_pg, b_pg, w_ple, ln3_g, ln3_b):
    bp, seq, _ = x_prompt.shape
    bs, dseq, _ = x_sample.shape
    n_p, n_s = bp * seq, bs * dseq
    n_tok = n_p + n_s
    assert seq % PROMPT_TILE == 0 and n_p % n_s == 0 and n_tok % MOE_TILE == 0
    assert bs & (bs - 1) == 0, "sample batch must be a power of two"

    w_in_b = w_in[0].astype(BF16)
    w_out_b = w_out[0].astype(BF16)
    pool_w_b = pool_w[0].astype(BF16)
    zero = jnp.zeros((RNN_HEAD, RNN_HEAD), F32)

    def pair_block(w, j):
        top = jnp.concatenate([w[2 * j], zero], axis=1)
        bot = jnp.concatenate([zero, w[2 * j + 1]], axis=1)
        return jnp.concatenate([top, bot], axis=0)

    gate_w = jnp.stack([jnp.concatenate([pair_block(rg_wa[0], j), pair_block(rg_wx[0], j)], axis=1)
                        for j in range(N_RNN_HEADS // 2)]).astype(BF16)
    wr = jnp.concatenate([w_rg[0].T, w_re[0].T,
                          jnp.zeros((ROUTER_ROWS - N_EXPERT_GROUPS - N_EXPERTS, D_MODEL), F32)], axis=0)
    wr_hi = wr.astype(BF16)
    wr_lo = (wr - wr_hi.astype(F32)).astype(BF16)
    br = jnp.concatenate([b_rg[0], b_re[0],
                          jnp.zeros((ROUTER_ROWS - N_EXPERT_GROUPS - N_EXPERTS,), F32)]).reshape(ROUTER_ROWS, 1)

    mixer_params = (_row(ln_in_g), _row(ln_in_b), w_in_b, pool_w_b, _row(pool_b[0]), _row(pool_scale[0]),
                    conv_w[0], _row(conv_b[0]), gate_w, _row(rg_ba[0]), _row(rg_bx[0]), _row(rg_lambda[0]),
                    w_out_b, _row(ln1_g[0]), _row(ln1_b[0]), wr_hi, wr_lo, br)
    mixer_specs = [_full(p.shape) for p in mixer_params]

    n_t = seq // PROMPT_TILE
    pay_p, route_p, cnt_p, pool_p, conv_p, h_p = pl.pallas_call(
        _prompt_kernel,
        grid=(bp, n_t),
        in_specs=[pl.BlockSpec((None, PROMPT_TILE, D_MODEL), lambda b, t: (b, t, 0)),
                  pl.BlockSpec((None, None, PROMPT_TILE, D_PLE), lambda b, t: (0, b, t, 0))] + mixer_specs,
        out_specs=[
            pl.BlockSpec((PROMPT_TILE, D_PAY), lambda b, t: (b * n_t + t, 0)),
            pl.BlockSpec((SUBLANES, PROMPT_TILE), lambda b, t: (0, b * n_t + t)),
            _full((CLASS_ROWS, 1)),
            pl.BlockSpec((None, POOL_HALO, D_POOL), lambda b, t: (b, 0, 0)),
            pl.BlockSpec((None, CONV_HALO, D_RNN), lambda b, t: (b, 0, 0)),
            pl.BlockSpec((None, SUBLANES, D_RNN), lambda b, t: (b, 0, 0)),
        ],
        out_shape=[
            jax.ShapeDtypeStruct((n_p, D_PAY), F32),
            jax.ShapeDtypeStruct((SUBLANES, n_p), F32),
            jax.ShapeDtypeStruct((CLASS_ROWS, 1), F32),
            jax.ShapeDtypeStruct((bp, POOL_HALO, D_POOL), F32),
            jax.ShapeDtypeStruct((bp, CONV_HALO, D_RNN), F32),
            jax.ShapeDtypeStruct((bp, SUBLANES, D_RNN), F32),
        ],
        scratch_shapes=[pltpu.VMEM((POOL_HALO, D_POOL), F32), pltpu.VMEM((CONV_HALO, D_RNN), F32),
                        pltpu.VMEM((1, D_RNN), F32), pltpu.VMEM((CLASS_ROWS, 1), F32)],
        compiler_params=pltpu.CompilerParams(dimension_semantics=("arbitrary", "arbitrary"),
                                             vmem_limit_bytes=VMEM_LIMIT),
        name="prompt_mixers",
    )(x_prompt, p_prompt, *mixer_params)

    xs_tm = x_sample.transpose(1, 0, 2).reshape(n_s, D_MODEL)
    pe_tm = p_sample[0].transpose(1, 0, 2).reshape(n_s, D_PLE)
    pool_buf_tm = state_pool[0].transpose(1, 0, 2)
    conv_buf_tm = state_conv[0].transpose(1, 0, 2)
    sample_in = (xs_tm, pe_tm, pool_buf_tm, conv_buf_tm, state_h[0], cnt_p)
    pay_s, route_s, cnt_all, pool_s, conv_s, h_s = pl.pallas_call(
        _sample_kernel,
        grid=(1,),
        in_specs=[_full(p.shape) for p in sample_in] + mixer_specs,
        out_specs=[_full((n_s, D_PAY)), _full((SUBLANES, n_s)),
                   _full((CLASS_ROWS, 1)), _full((POOL_BUF, bs, D_POOL)),
                   _full((CONV_WIDTH - 1, bs, D_RNN)), _full((bs, D_RNN))],
        out_shape=[
            jax.ShapeDtypeStruct((n_s, D_PAY), F32),
            jax.ShapeDtypeStruct((SUBLANES, n_s), F32),
            jax.ShapeDtypeStruct((CLASS_ROWS, 1), F32),
            jax.ShapeDtypeStruct((POOL_BUF, bs, D_POOL), F32),
            jax.ShapeDtypeStruct((CONV_WIDTH - 1, bs, D_RNN), F32),
            jax.ShapeDtypeStruct((bs, D_RNN), F32),
        ],
        compiler_params=pltpu.CompilerParams(dimension_semantics=("arbitrary",), vmem_limit_bytes=VMEM_LIMIT),
        name="sample_mixers",
    )(*sample_in, *mixer_params)

    tm = MOE_TILE
    n_tiles = n_tok // tm + N_CLASSES
    counts = cnt_all[:N_CLASSES, 0].astype(I32)
    counts_p = cnt_p[:N_CLASSES, 0].astype(I32)
    tiles_c = (counts + tm - 1) // tm
    tile_end = jnp.cumsum(tiles_c)
    tile_start = tile_end - tiles_c
    row_start = tile_start * tm
    cls_all = jnp.concatenate([route_p[0], route_s[0]]).astype(I32)
    rank_all = jnp.concatenate([route_p[1], route_s[1]]).astype(I32)
    dest = row_start[cls_all] + rank_all
    tile_id = jnp.arange(n_tiles, dtype=I32)
    tile_cls = jnp.minimum(jnp.sum((tile_id[:, None] >= tile_end[None, :]).astype(I32), axis=1), N_CLASSES - 1)
    used = tile_id < tile_end[-1]
    first_row = (tile_id - tile_start[tile_cls]) * tm
    nv = jnp.where(used, jnp.clip(counts[tile_cls] - first_row, 0, tm), 0).astype(I32)
    nvp = jnp.minimum(jnp.clip(counts_p[tile_cls] - first_row, 0, tm), nv).astype(I32)
    pair_lo = jnp.array([p[0] for p in PAIRS], I32)
    pair_hi = jnp.array([p[1] for p in PAIRS], I32)
    tile_group = tile_cls // len(PAIRS)
    ea = (tile_group * EXPERTS_PER_GROUP + pair_lo[tile_cls % len(PAIRS)]).astype(I32)
    eb = (tile_group * EXPERTS_PER_GROUP + pair_hi[tile_cls % len(PAIRS)]).astype(I32)

    moe_params = (_row(ln2_g[0]), _row(ln2_b[0]), w_pg[0].astype(BF16), _row(b_pg[0]), w_ple[0].astype(BF16),
                  _row(ln3_g[0]), _row(ln3_b[0]))
    w_eg_b, w_eu_b, w_ed_b = w_eg[0].astype(BF16), w_eu[0].astype(BF16), w_ed[0].astype(BF16)
    up_spec_a = pl.BlockSpec((None, D_MODEL, D_EXPERT), lambda t, d, n, p, a, b: (a[t], 0, 0))
    up_spec_b = pl.BlockSpec((None, D_MODEL, D_EXPERT), lambda t, d, n, p, a, b: (b[t], 0, 0))
    dn_spec_a = pl.BlockSpec((None, D_EXPERT, D_MODEL), lambda t, d, n, p, a, b: (a[t], 0, 0))
    dn_spec_b = pl.BlockSpec((None, D_EXPERT, D_MODEL), lambda t, d, n, p, a, b: (b[t], 0, 0))
    y_p, y_s, _ = pl.pallas_call(
        functools.partial(_moe_kernel, sample_batch=bs, sample_seq=dseq),
        grid_spec=pltpu.PrefetchScalarGridSpec(
            num_scalar_prefetch=5,
            grid=(n_tiles,),
            in_specs=[pl.BlockSpec(memory_space=pl.ANY), pl.BlockSpec(memory_space=pl.ANY),
                      up_spec_a, up_spec_a, dn_spec_a,
                      up_spec_b, up_spec_b, dn_spec_b] + [_full(p.shape) for p in moe_params],
            out_specs=[pl.BlockSpec(memory_space=pl.ANY)] * 3,
            scratch_shapes=[pltpu.SMEM((n_tiles * tm,), I32),
                            pltpu.VMEM((2, tm, D_PAY), F32), pltpu.VMEM((2, tm, D_MODEL), F32),
                            pltpu.SemaphoreType.DMA((2,)), pltpu.SemaphoreType.DMA((2,))],
        ),
        out_shape=[jax.ShapeDtypeStruct((n_p, D_MODEL), F32), jax.ShapeDtypeStruct((n_s, D_MODEL), F32),
                   jax.ShapeDtypeStruct((2 * SUBLANES, D_MODEL), F32)],
        compiler_params=pltpu.CompilerParams(dimension_semantics=("arbitrary",), vmem_limit_bytes=VMEM_LIMIT),
        name="moe_embed",
    )(dest, nv, nvp, ea, eb, pay_p, pay_s, w_eg_b, w_eu_b, w_ed_b, w_eg_b, w_eu_b, w_ed_b, *moe_params)

    return (y_p.reshape(bp, seq, D_MODEL), y_s.reshape(bs, dseq, D_MODEL),
            pool_p[None, :, POOL_HALO - POOL_BUF:, :], conv_p[None, :, CONV_HALO - (CONV_WIDTH - 1):, :],
            h_p[None, :, 0, :],
            pool_s.transpose(1, 0, 2)[None], conv_s.transpose(1, 0, 2)[None], h_s[None])
```

```python
import functools

import jax
import jax.numpy as jnp
from jax import lax
from jax.experimental import pallas as pl
from jax.experimental.pallas import tpu as pltpu

F32 = jnp.float32
BF16 = jnp.bfloat16
I32 = jnp.int32

D_MODEL = 1024
D_POOL = 512
D_RNN = 512
POOL_WINDOWS = (2, 4, 8, 16)
POOL_GROUP = 128
POOL_BUF = 15
N_RNN_HEADS = 8
RNN_HEAD = 64
CONV_WIDTH = 4
LRU_C = 8.0
N_EXPERT_GROUPS = 4
EXPERTS_PER_GROUP = 4
N_EXPERTS = 16
D_EXPERT = 512
D_PLE = 256
LN_EPS = 1e-5
DEPTH = 1
ALPHA = (2 * DEPTH) ** 0.25
PAST_LEN = 16384

LANES = 128
SUBLANES = 8
ROUTER_ROWS = 32
PAIRS = ((0, 1), (0, 2), (0, 3), (1, 2), (1, 3), (2, 3))
N_CLASSES = N_EXPERT_GROUPS * len(PAIRS)
CLASS_ROWS = 32
POOL_HALO = 16
CONV_HALO = 8
PROMPT_TILE = 256
MOE_TILE = 256
UNROLL = SUBLANES
UNROLL_LOG2 = 3
D_PAY = D_MODEL + LANES + D_PLE
VMEM_LIMIT = 56 * 1024 * 1024


def _layer_norm(x, g, b):
    mu = jnp.mean(x, axis=-1, keepdims=True)
    xc = x - mu
    var = jnp.mean(xc * xc, axis=-1, keepdims=True)
    return xc * lax.rsqrt(var + LN_EPS) * g + b


def _bdot(a, w):
    return jnp.dot(a.astype(BF16), w, preferred_element_type=F32)


def _gates(xc, gate_w_ref, ba, bx):
    r_parts, i_parts = [], []
    for j in range(D_RNN // LANES):
        o = _bdot(xc[:, LANES * j:LANES * (j + 1)], gate_w_ref[j])
        r_parts.append(o[:, :LANES])
        i_parts.append(o[:, LANES:])
    r = jax.nn.sigmoid(jnp.concatenate(r_parts, axis=-1) + ba)
    i = jax.nn.sigmoid(jnp.concatenate(i_parts, axis=-1) + bx)
    return r, i


def _lru_terms(xc, r, i, lam, first_pos_mask):
    log_a = -LRU_C * r * jax.nn.softplus(-lam)
    a = jnp.exp(log_a)
    mult = jnp.sqrt(jnp.tanh(-log_a) * (a * a + 1.0))
    if first_pos_mask is not None:
        mult = jnp.where(first_pos_mask, 1.0, mult)
    return a, mult * i * xc


def _pool_project(d, pool_w_ref, pool_b, pool_scale):
    outs = []
    for g in range(len(POOL_WINDOWS)):
        outs.append(_bdot(d[:, POOL_GROUP * g:POOL_GROUP * (g + 1)], pool_w_ref[g]))
    return (jnp.concatenate(outs, axis=-1) + pool_b) * pool_scale


def _route(x1, wr_hi_ref, wr_lo_ref, br):
    hi = x1.astype(BF16)
    lo = (x1 - hi.astype(F32)).astype(BF16)
    nt = (((1,), (1,)), ((), ()))
    logits = (lax.dot_general(wr_hi_ref[...], hi, nt, preferred_element_type=F32)
              + lax.dot_general(wr_hi_ref[...], lo, nt, preferred_element_type=F32)
              + lax.dot_general(wr_lo_ref[...], hi, nt, preferred_element_type=F32)) + br
    lg = [logits[j:j + 1, :] for j in range(N_EXPERT_GROUPS)]
    m = functools.reduce(jnp.maximum, lg)
    ex = [jnp.exp(v - m) for v in lg]
    den = functools.reduce(lambda p, q: p + q, ex)
    gp = [v / den for v in ex]
    g_w, g_idx = gp[0], jnp.zeros_like(gp[0], dtype=I32)
    for j in range(1, N_EXPERT_GROUPS):
        upd = gp[j] > g_w
        g_idx = jnp.where(upd, j, g_idx)
        g_w = jnp.where(upd, gp[j], g_w)
    e_in = []
    for k in range(EXPERTS_PER_GROUP):
        v = jnp.zeros_like(g_w)
        for g in range(N_EXPERT_GROUPS):
            row = N_EXPERT_GROUPS + g * EXPERTS_PER_GROUP + k
            v = v + jnp.where(g_idx == g, logits[row:row + 1, :], 0.0)
        e_in.append(v)
    m = functools.reduce(jnp.maximum, e_in)
    ex = [jnp.exp(v - m) for v in e_in]
    den = functools.reduce(lambda p, q: p + q, ex)
    ep = [v / den for v in ex]
    w0, i0 = ep[0], jnp.zeros_like(g_idx)
    for k in range(1, EXPERTS_PER_GROUP):
        upd = ep[k] > w0
        i0 = jnp.where(upd, k, i0)
        w0 = jnp.where(upd, ep[k], w0)
    w1, i1 = jnp.full_like(w0, -1.0), jnp.zeros_like(g_idx)
    for k in range(EXPERTS_PER_GROUP):
        upd = (ep[k] > w1) & (i0 != k)
        i1 = jnp.where(upd, k, i1)
        w1 = jnp.where(upd, ep[k], w1)
    wsum = w0 + w1
    c0 = g_w * (w0 / wsum)
    c1 = g_w * (w1 / wsum)
    first_is_lo = i0 < i1
    e_lo = jnp.minimum(i0, i1)
    e_hi = jnp.maximum(i0, i1)
    pair = jnp.zeros_like(g_idx)
    for p, (a, b) in enumerate(PAIRS):
        pair = jnp.where((e_lo == a) & (e_hi == b), p, pair)
    cls = g_idx * len(PAIRS) + pair
    return cls, jnp.where(first_is_lo, c0, c1), jnp.where(first_is_lo, c1, c0)


def _rank_in_class(cls, cnt_ref):
    t = cls.shape[1]
    onehot = (lax.broadcasted_iota(I32, (CLASS_ROWS, t), 0) == cls).astype(F32)
    before = (lax.broadcasted_iota(I32, (t, t), 0) < lax.broadcasted_iota(I32, (t, t), 1)).astype(BF16)
    prefix = jnp.dot(onehot.astype(BF16), before, preferred_element_type=F32)
    rank = jnp.sum(onehot * (prefix + cnt_ref[...]), axis=0, keepdims=True)
    cnt_ref[...] = cnt_ref[...] + jnp.sum(onehot, axis=1, keepdims=True)
    return rank


def _write_token_rows(pay_ref, route_ref, x1, pe, cls, rank, c_lo, c_hi):
    t = x1.shape[0]
    meta_t = jnp.concatenate([c_lo, c_hi, jnp.zeros((LANES - 2, t), F32)], axis=0)
    pay_ref[:, :D_MODEL] = x1
    pay_ref[:, D_MODEL:D_MODEL + LANES] = meta_t.T
    pay_ref[:, D_MODEL + LANES:] = pe
    route_ref[...] = jnp.concatenate([cls.astype(F32), rank, jnp.zeros((SUBLANES - 2, t), F32)], axis=0)


def _shift_rows(x, s):
    return pltpu.roll(x, s, 0)


def _prompt_kernel(x_ref, pe_ref, ln_in_g, ln_in_b, w_in, pool_w, pool_b, pool_scale, conv_w, conv_b,
                   gate_w, ba, bx, lam, w_out, ln1_g, ln1_b, wr_hi, wr_lo, br,
                   pay_ref, route_ref, cnt_out_ref, pool_new_ref, conv_new_ref, h_new_ref,
                   pool_halo, conv_halo, h_carry, cnt):
    t_idx = pl.program_id(1)
    tt = x_ref.shape[0]

    @pl.when((pl.program_id(0) == 0) & (t_idx == 0))
    def _():
        cnt[...] = jnp.zeros_like(cnt)

    @pl.when(t_idx == 0)
    def _():
        pool_halo[...] = jnp.zeros_like(pool_halo)
        conv_halo[...] = jnp.zeros_like(conv_halo)
        h_carry[...] = jnp.zeros_like(h_carry)

    xn = _layer_norm(x_ref[...], ln_in_g[...], ln_in_b[...])
    proj = _bdot(xn, w_in[...])
    u = proj[:, :D_POOL]
    xr = proj[:, D_POOL:D_POOL + D_RNN]
    gate = proj[:, D_POOL + D_RNN:]

    row = lax.broadcasted_iota(I32, (tt, 1), 0)
    pos = t_idx * tt + row

    e = jnp.concatenate([pool_halo[...], u], axis=0)
    s = e + _shift_rows(e, 1)
    sums = [s[:, :POOL_GROUP]]
    s = s[:, POOL_GROUP:]
    for step in (2, 4, 8):
        s = s + _shift_rows(s, step)
        sums.append(s[:, :POOL_GROUP])
        s = s[:, POOL_GROUP:]
    d_parts = []
    for g, win in enumerate(POOL_WINDOWS):
        cnt_w = jnp.minimum(pos + 1, win).astype(F32)
        d_parts.append(sums[g][POOL_HALO:, :] / cnt_w - u[:, POOL_GROUP * g:POOL_GROUP * (g + 1)])
    pool_out = _pool_project(jnp.concatenate(d_parts, axis=-1), pool_w, pool_b[...], pool_scale[...])
    pool_halo[...] = u[tt - POOL_HALO:, :]

    ec = jnp.concatenate([conv_halo[...], xr], axis=0)
    xc = conv_b[...] + _shift_rows(ec, 3) * conv_w[0:1, :]
    xc = xc + _shift_rows(ec, 2) * conv_w[1:2, :]
    xc = xc + _shift_rows(ec, 1) * conv_w[2:3, :]
    xc = (xc + ec * conv_w[3:4, :])[CONV_HALO:, :]
    conv_halo[...] = xr[tt - CONV_HALO:, :]

    r, i = _gates(xc, gate_w, ba[...], bx[...])
    a, b = _lru_terms(xc, r, i, lam[...], pos == 0)
    b = b + jnp.where(row == 0, a * h_carry[...], 0.0)
    step = 1
    while step < tt:
        keep = row >= step
        a_prev = jnp.where(keep, _shift_rows(a, step), 1.0)
        b_prev = jnp.where(keep, _shift_rows(b, step), 0.0)
        b = a * b_prev + b
        a = a * a_prev
        step *= 2
    h = b
    h_carry[...] = h[tt - 1:tt, :]
    rnn_out = h * jax.nn.gelu(gate)

    mix = _bdot(jnp.concatenate([pool_out, rnn_out], axis=-1), w_out[...])
    x1 = _layer_norm(ALPHA * xn + mix, ln1_g[...], ln1_b[...])
    cls, c_lo, c_hi = _route(x1, wr_hi, wr_lo, br[...])
    rank = _rank_in_class(cls, cnt)
    _write_token_rows(pay_ref, route_ref, x1, pe_ref[...], cls, rank, c_lo, c_hi)
    cnt_out_ref[...] = cnt[...]

    @pl.when(t_idx == pl.num_programs(1) - 1)
    def _():
        pool_new_ref[...] = u[tt - POOL_HALO:, :]
        conv_new_ref[...] = xr[tt - CONV_HALO:, :]
        h_new_ref[...] = jnp.broadcast_to(h[tt - 1:tt, :], h_new_ref.shape)


def _sample_kernel(x_ref, pe_ref, pool_buf_ref, conv_buf_ref, h0_ref, cnt_in_ref,
                   ln_in_g, ln_in_b, w_in, pool_w, pool_b, pool_scale, conv_w, conv_b,
                   gate_w, ba, bx, lam, w_out, ln1_g, ln1_b, wr_hi, wr_lo, br,
                   pay_ref, route_ref, cnt_out_ref, pool_new_ref, conv_new_ref, h_new_ref):
    nb = h0_ref.shape[0]
    t_len = x_ref.shape[0] // nb
    xn = _layer_norm(x_ref[...], ln_in_g[...], ln_in_b[...])
    proj = _bdot(xn, w_in[...])
    u = proj[:, :D_POOL]
    xr = proj[:, D_POOL:D_POOL + D_RNN]
    gate = proj[:, D_POOL + D_RNN:]

    f = [pool_buf_ref[k] for k in range(POOL_BUF)] + [u[nb * t:nb * (t + 1), :] for t in range(t_len)]
    n_rows = len(f)
    s = [f[0]] + [f[k] + f[k - 1] for k in range(1, n_rows)]
    sums = [[v[:, :POOL_GROUP] for v in s]]
    s = [v[:, POOL_GROUP:] for v in s]
    for step in (2, 4, 8):
        s = [s[k] + s[k - step] if k >= step else s[k] for k in range(n_rows)]
        sums.append([v[:, :POOL_GROUP] for v in s])
        s = [v[:, POOL_GROUP:] for v in s]
    d_rows = []
    for t in range(t_len):
        parts = []
        for g, win in enumerate(POOL_WINDOWS):
            cnt_w = float(min(PAST_LEN + t + 1, win))
            parts.append(sums[g][POOL_BUF + t] / cnt_w - f[POOL_BUF + t][:, POOL_GROUP * g:POOL_GROUP * (g + 1)])
        d_rows.append(jnp.concatenate(parts, axis=-1))
    pool_out = _pool_project(jnp.concatenate(d_rows, axis=0), pool_w, pool_b[...], pool_scale[...])
    for k in range(POOL_BUF):
        pool_new_ref[k] = f[n_rows - POOL_BUF + k]

    gbuf = [conv_buf_ref[k] for k in range(CONV_WIDTH - 1)] + [xr[nb * t:nb * (t + 1), :] for t in range(t_len)]
    xc_rows = []
    for t in range(t_len):
        v = conv_b[...] + gbuf[t] * conv_w[0:1, :]
        for k in range(1, CONV_WIDTH):
            v = v + gbuf[t + k] * conv_w[k:k + 1, :]
        xc_rows.append(v)
    xc = jnp.concatenate(xc_rows, axis=0)
    for k in range(CONV_WIDTH - 1):
        conv_new_ref[k] = gbuf[len(gbuf) - (CONV_WIDTH - 1) + k]

    r, i = _gates(xc, gate_w, ba[...], bx[...])
    a, b = _lru_terms(xc, r, i, lam[...], None)
    h = h0_ref[...]
    h_rows = []
    for t in range(t_len):
        h = a[nb * t:nb * (t + 1), :] * h + b[nb * t:nb * (t + 1), :]
        h_rows.append(h)
    h_new_ref[...] = h
    rnn_out = jnp.concatenate(h_rows, axis=0) * jax.nn.gelu(gate)

    mix = _bdot(jnp.concatenate([pool_out, rnn_out], axis=-1), w_out[...])
    x1 = _layer_norm(ALPHA * xn + mix, ln1_g[...], ln1_b[...])
    cls, c_lo, c_hi = _route(x1, wr_hi, wr_lo, br[...])
    cnt_out_ref[...] = cnt_in_ref[...]
    rank = _rank_in_class(cls, cnt_out_ref)
    _write_token_rows(pay_ref, route_ref, x1, pe_ref[...], cls, rank, c_lo, c_hi)


def _moe_kernel(dest_ref, nv_ref, nvp_ref, ea_ref, eb_ref,
                payp_hbm, pays_hbm, wg_a, wu_a, wd_a, wg_b, wu_b, wd_b,
                ln2_g, ln2_b, w_pg, b_pg, w_ple, ln3_g, ln3_b,
                yp_hbm, ys_hbm, trash_hbm,
                src, gbuf, obuf, gsem, ssem, *, sample_batch, sample_seq):
    del ea_ref, eb_ref
    t = pl.program_id(0)
    n_tiles = pl.num_programs(0)
    tm = gbuf.shape[1] * SUBLANES
    slot = t % 2
    n_tok = dest_ref.shape[0]
    n_prompt = payp_hbm.shape[0]
    log2_batch = sample_batch.bit_length() - 1

    def rows8(tile):
        return pl.multiple_of((nv_ref[tile] + (SUBLANES - 1)) & -SUBLANES, SUBLANES)

    def buf_row(buf, slot_, j):
        return buf.at[slot_, lax.shift_right_logical(j, UNROLL_LOG2), pl.ds(j & (SUBLANES - 1), 1), :]

    def for_each_row(lo, hi, per_row):
        def row_body(j, c):
            per_row(j)
            return c
        lax.fori_loop(lo, hi, row_body, 0)

    def for_rows(hi, per_group, per_row):
        n_groups = lax.shift_right_logical(hi, UNROLL_LOG2)

        def group_body(g, c):
            per_group(g)
            return c
        lax.fori_loop(0, n_groups, group_body, 0)
        for_each_row(lax.shift_left(n_groups, UNROLL_LOG2), hi, per_row)

    def issue_gather(tile, slot_):
        base = tile * tm

        def from_prompt_group(g):
            for k in range(UNROLL):
                pltpu.make_async_copy(payp_hbm.at[pl.ds(src[base + g * UNROLL + k], 1), :],
                                      gbuf.at[slot_, g, pl.ds(k, 1), :], gsem.at[slot_]).start()

        def from_prompt(j):
            pltpu.make_async_copy(payp_hbm.at[pl.ds(src[base + j], 1), :], buf_row(gbuf, slot_, j),
                                  gsem.at[slot_]).start()

        def from_sample(j):
            pltpu.make_async_copy(pays_hbm.at[pl.ds(src[base + j] - n_prompt, 1), :], buf_row(gbuf, slot_, j),
                                  gsem.at[slot_]).start()

        def rounding(j):
            pltpu.make_async_copy(payp_hbm.at[pl.ds(0, 1), :], buf_row(gbuf, slot_, j), gsem.at[slot_]).start()

        nvp_, nv_ = nvp_ref[tile], nv_ref[tile]
        for_rows(nvp_, from_prompt_group, from_prompt)
        for_each_row(nvp_, nv_, from_sample)
        for_each_row(nv_, rows8(tile), rounding)

    def wait_gather(tile, slot_):
        n = rows8(tile)

        @pl.when(n > 0)
        def _():
            rows = payp_hbm.at[pl.ds(0, n), :]
            pltpu.make_async_copy(rows, rows, gsem.at[slot_]).wait()

    def wait_scatter(tile, slot_):
        n = rows8(tile)

        @pl.when(n > 0)
        def _():
            rows = yp_hbm.at[pl.ds(0, n), :]
            pltpu.make_async_copy(rows, rows, ssem.at[slot_]).wait()

    @pl.when(t == 0)
    def _():
        gbuf[...] = jnp.zeros_like(gbuf)
        init = pltpu.make_async_copy(gbuf.at[0, pl.ds(0, 2), :, pl.ds(0, D_MODEL)], trash_hbm, ssem.at[0])
        init.start()
        init.wait()

        def fill(g, c):
            for k in range(UNROLL):
                i = g * UNROLL + k
                src[dest_ref[i]] = i
            return c
        lax.fori_loop(0, n_tok // UNROLL, fill, 0)
        issue_gather(0, 0)

    @pl.when(t + 1 < n_tiles)
    def _():
        issue_gather(t + 1, 1 - slot)

    wait_gather(t, slot)

    @pl.when(t >= 2)
    def _():
        wait_scatter(t - 2, slot)

    nv = nv_ref[t]

    @pl.when(nv > 0)
    def _():
        xp = jnp.concatenate([gbuf[slot, g] for g in range(tm // SUBLANES)], axis=0)
        x1 = xp[:, :D_MODEL]
        meta = xp[:, D_MODEL:D_MODEL + LANES]
        pe = xp[:, D_MODEL + LANES:]
        xb = x1.astype(BF16)

        def expert(wg, wu, wd):
            hid = jax.nn.silu(jnp.dot(xb, wg[...], preferred_element_type=F32)) * jnp.dot(
                xb, wu[...], preferred_element_type=F32)
            return _bdot(hid, wd[...])

        moe = meta[:, 0:1] * expert(wg_a, wu_a, wd_a)
        moe = moe + meta[:, 1:2] * expert(wg_b, wu_b, wd_b)
        x2 = _layer_norm(ALPHA * x1 + moe, ln2_g[...], ln2_b[...])
        gate_p = jax.nn.sigmoid(_bdot(x2, w_pg[...]) + b_pg[...])
        ple = _bdot(pe, w_ple[...]) * gate_p
        y = _layer_norm(ALPHA * x2 + ple, ln3_g[...], ln3_b[...])
        for g in range(tm // SUBLANES):
            obuf[slot, g] = y[SUBLANES * g:SUBLANES * (g + 1), :]

    base = t * tm
    nvp = nvp_ref[t]

    def to_prompt_group(g):
        for k in range(UNROLL):
            pltpu.make_async_copy(obuf.at[slot, g, pl.ds(k, 1), :],
                                  yp_hbm.at[pl.ds(src[base + g * UNROLL + k], 1), :], ssem.at[slot]).start()

    def to_prompt(j):
        pltpu.make_async_copy(buf_row(obuf, slot, j), yp_hbm.at[pl.ds(src[base + j], 1), :], ssem.at[slot]).start()

    def to_sample(j):
        s = src[base + j] - n_prompt
        r = (s & (sample_batch - 1)) * sample_seq + lax.shift_right_logical(s, log2_batch)
        pltpu.make_async_copy(buf_row(obuf, slot, j), ys_hbm.at[pl.ds(r, 1), :], ssem.at[slot]).start()

    def to_trash(j):
        pltpu.make_async_copy(buf_row(obuf, slot, j), trash_hbm.at[slot, pl.ds(j - nv, 1), :], ssem.at[slot]).start()

    for_rows(nvp, to_prompt_group, to_prompt)
    for_each_row(nvp, nv, to_sample)
    for_each_row(nv, rows8(t), to_trash)

    @pl.when(t == n_tiles - 1)
    def _():
        @pl.when(t >= 1)
        def _():
            wait_scatter(t - 1, 1 - slot)
        wait_scatter(t, slot)


def _full(shape):
    nd = len(shape)
    return pl.BlockSpec(shape, lambda *_: (0,) * nd)


def _row(v):
    return v.reshape(1, -1).astype(F32)


def kernel(x_prompt, x_sample, state_pool, state_conv, state_h, p_prompt, p_sample, ln_in_g, ln_in_b, w_in,
           pool_w, pool_b, pool_scale, conv_w, conv_b, rg_wa, rg_ba, rg_wx, rg_bx, rg_lambda, w_out, ln1_g,
           ln1_b, w_rg, b_rg, w_re, b_re, w_eg, w_eu, w_ed, ln2_g, ln2_b, w_pg, b_pg, w_ple, ln3_g, ln3_b):
    bp, seq, _ = x_prompt.shape
    bs, dseq, _ = x_sample.shape
    n_p, n_s = bp * seq, bs * dseq
    n_tok = n_p + n_s
    assert seq % PROMPT_TILE == 0 and n_p % n_s == 0 and n_tok % MOE_TILE == 0
    assert bs & (bs - 1) == 0, "sample batch must be a power of two"

    w_in_b = w_in[0].astype(BF16)
    w_out_b = w_out[0].astype(BF16)
    pool_w_b = pool_w[0].astype(BF16)
    zero = jnp.zeros((RNN_HEAD, RNN_HEAD), F32)

    def pair_block(w, j):
        top = jnp.concatenate([w[2 * j], zero], axis=1)
        bot = jnp.concatenate([zero, w[2 * j + 1]], axis=1)
        return jnp.concatenate([top, bot], axis=0)

    gate_w = jnp.stack([jnp.concatenate([pair_block(rg_wa[0], j), pair_block(rg_wx[0], j)], axis=1)
                        for j in range(N_RNN_HEADS // 2)]).astype(BF16)
    wr = jnp.concatenate([w_rg[0].T, w_re[0].T,
                          jnp.zeros((ROUTER_ROWS - N_EXPERT_GROUPS - N_EXPERTS, D_MODEL), F32)], axis=0)
    wr_hi = wr.astype(BF16)
    wr_lo = (wr - wr_hi.astype(F32)).astype(BF16)
    br = jnp.concatenate([b_rg[0], b_re[0],
                          jnp.zeros((ROUTER_ROWS - N_EXPERT_GROUPS - N_EXPERTS,), F32)]).reshape(ROUTER_ROWS, 1)

    mixer_params = (_row(ln_in_g), _row(ln_in_b), w_in_b, pool_w_b, _row(pool_b[0]), _row(pool_scale[0]),
                    conv_w[0], _row(conv_b[0]), gate_w, _row(rg_ba[0]), _row(rg_bx[0]), _row(rg_lambda[0]),
                    w_out_b, _row(ln1_g[0]), _row(ln1_b[0]), wr_hi, wr_lo, br)
    mixer_specs = [_full(p.shape) for p in mixer_params]

    n_t = seq // PROMPT_TILE
    pay_p, route_p, cnt_p, pool_p, conv_p, h_p = pl.pallas_call(
        _prompt_kernel,
        grid=(bp, n_t),
        in_specs=[pl.BlockSpec((None, PROMPT_TILE, D_MODEL), lambda b, t: (b, t, 0)),
                  pl.BlockSpec((None, None, PROMPT_TILE, D_PLE), lambda b, t: (0, b, t, 0))] + mixer_specs,
        out_specs=[
            pl.BlockSpec((PROMPT_TILE, D_PAY), lambda b, t: (b * n_t + t, 0)),
            pl.BlockSpec((SUBLANES, PROMPT_TILE), lambda b, t: (0, b * n_t + t)),
            _full((CLASS_ROWS, 1)),
            pl.BlockSpec((None, POOL_HALO, D_POOL), lambda b, t: (b, 0, 0)),
            pl.BlockSpec((None, CONV_HALO, D_RNN), lambda b, t: (b, 0, 0)),
            pl.BlockSpec((None, SUBLANES, D_RNN), lambda b, t: (b, 0, 0)),
        ],
        out_shape=[
            jax.ShapeDtypeStruct((n_p, D_PAY), F32),
            jax.ShapeDtypeStruct((SUBLANES, n_p), F32),
            jax.ShapeDtypeStruct((CLASS_ROWS, 1), F32),
            jax.ShapeDtypeStruct((bp, POOL_HALO, D_POOL), F32),
            jax.ShapeDtypeStruct((bp, CONV_HALO, D_RNN), F32),
            jax.ShapeDtypeStruct((bp, SUBLANES, D_RNN), F32),
        ],
        scratch_shapes=[pltpu.VMEM((POOL_HALO, D_POOL), F32), pltpu.VMEM((CONV_HALO, D_RNN), F32),
                        pltpu.VMEM((1, D_RNN), F32), pltpu.VMEM((CLASS_ROWS, 1), F32)],
        compiler_params=pltpu.CompilerParams(dimension_semantics=("arbitrary", "arbitrary"),
                                             vmem_limit_bytes=VMEM_LIMIT),
        name="prompt_mixers",
    )(x_prompt, p_prompt, *mixer_params)

    xs_tm = x_sample.transpose(1, 0, 2).reshape(n_s, D_MODEL)
    pe_tm = p_sample[0].transpose(1, 0, 2).reshape(n_s, D_PLE)
    pool_buf_tm = state_pool[0].transpose(1, 0, 2)
    conv_buf_tm = state_conv[0].transpose(1, 0, 2)
    sample_in = (xs_tm, pe_tm, pool_buf_tm, conv_buf_tm, state_h[0], cnt_p)
    pay_s, route_s, cnt_all, pool_s, conv_s, h_s = pl.pallas_call(
        _sample_kernel,
        grid=(1,),
        in_specs=[_full(p.shape) for p in sample_in] + mixer_specs,
        out_specs=[_full((n_s, D_PAY)), _full((SUBLANES, n_s)),
                   _full((CLASS_ROWS, 1)), _full((POOL_BUF, bs, D_POOL)),
                   _full((CONV_WIDTH - 1, bs, D_RNN)), _full((bs, D_RNN))],
        out_shape=[
            jax.ShapeDtypeStruct((n_s, D_PAY), F32),
            jax.ShapeDtypeStruct((SUBLANES, n_s), F32),
            jax.ShapeDtypeStruct((CLASS_ROWS, 1), F32),
            jax.ShapeDtypeStruct((POOL_BUF, bs, D_POOL), F32),
            jax.ShapeDtypeStruct((CONV_WIDTH - 1, bs, D_RNN), F32),
            jax.ShapeDtypeStruct((bs, D_RNN), F32),
        ],
        compiler_params=pltpu.CompilerParams(dimension_semantics=("arbitrary",), vmem_limit_bytes=VMEM_LIMIT),
        name="sample_mixers",
    )(*sample_in, *mixer_params)

    tm = MOE_TILE
    n_tiles = n_tok // tm + N_CLASSES
    counts = cnt_all[:N_CLASSES, 0].astype(I32)
    counts_p = cnt_p[:N_CLASSES, 0].astype(I32)
    tiles_c = (counts + tm - 1) // tm
    tile_end = jnp.cumsum(tiles_c)
    tile_start = tile_end - tiles_c
    row_start = tile_start * tm
    cls_all = jnp.concatenate([route_p[0], route_s[0]]).astype(I32)
    rank_all = jnp.concatenate([route_p[1], route_s[1]]).astype(I32)
    dest = row_start[cls_all] + rank_all
    tile_id = jnp.arange(n_tiles, dtype=I32)
    tile_cls = jnp.minimum(jnp.sum((tile_id[:, None] >= tile_end[None, :]).astype(I32), axis=1), N_CLASSES - 1)
    used = tile_id < tile_end[-1]
    first_row = (tile_id - tile_start[tile_cls]) * tm
    nv = jnp.where(used, jnp.clip(counts[tile_cls] - first_row, 0, tm), 0).astype(I32)
    nvp = jnp.minimum(jnp.clip(counts_p[tile_cls] - first_row, 0, tm), nv).astype(I32)
    pair_lo = jnp.array([p[0] for p in PAIRS], I32)
    pair_hi = jnp.array([p[1] for p in PAIRS], I32)
    tile_group = tile_cls // len(PAIRS)
    ea = (tile_group * EXPERTS_PER_GROUP + pair_lo[tile_cls % len(PAIRS)]).astype(I32)
    eb = (tile_group * EXPERTS_PER_GROUP + pair_hi[tile_cls % len(PAIRS)]).astype(I32)

    moe_params = (_row(ln2_g[0]), _row(ln2_b[0]), w_pg[0].astype(BF16), _row(b_pg[0]), w_ple[0].astype(BF16),
                  _row(ln3_g[0]), _row(ln3_b[0]))
    w_eg_b, w_eu_b, w_ed_b = w_eg[0].astype(BF16), w_eu[0].astype(BF16), w_ed[0].astype(BF16)
    up_spec_a = pl.BlockSpec((None, D_MODEL, D_EXPERT), lambda t, d, n, p, a, b: (a[t], 0, 0))
    up_spec_b = pl.BlockSpec((None, D_MODEL, D_EXPERT), lambda t, d, n, p, a, b: (b[t], 0, 0))
    dn_spec_a = pl.BlockSpec((None, D_EXPERT, D_MODEL), lambda t, d, n, p, a, b: (a[t], 0, 0))
    dn_spec_b = pl.BlockSpec((None, D_EXPERT, D_MODEL), lambda t, d, n, p, a, b: (b[t], 0, 0))
    y_p, y_s, _ = pl.pallas_call(
        functools.partial(_moe_kernel, sample_batch=bs, sample_seq=dseq),
        grid_spec=pltpu.PrefetchScalarGridSpec(
            num_scalar_prefetch=5,
            grid=(n_tiles,),
            in_specs=[pl.BlockSpec(memory_space=pl.ANY), pl.BlockSpec(memory_space=pl.ANY),
                      up_spec_a, up_spec_a, dn_spec_a,
                      up_spec_b, up_spec_b, dn_spec_b] + [_full(p.shape) for p in moe_params],
            out_specs=[pl.BlockSpec(memory_space=pl.ANY)] * 3,
            scratch_shapes=[pltpu.SMEM((n_tiles * tm,), I32),
                            pltpu.VMEM((2, tm // SUBLANES, SUBLANES, D_PAY), F32),
                            pltpu.VMEM((2, tm // SUBLANES, SUBLANES, D_MODEL), F32),
                            pltpu.SemaphoreType.DMA((2,)), pltpu.SemaphoreType.DMA((2,))],
        ),
        out_shape=[jax.ShapeDtypeStruct((n_p, D_MODEL), F32), jax.ShapeDtypeStruct((n_s, D_MODEL), F32),
                   jax.ShapeDtypeStruct((2, SUBLANES, D_MODEL), F32)],
        compiler_params=pltpu.CompilerParams(dimension_semantics=("arbitrary",), vmem_limit_bytes=VMEM_LIMIT),
        name="moe_embed",
    )(dest, nv, nvp, ea, eb, pay_p, pay_s, w_eg_b, w_eu_b, w_ed_b, w_eg_b, w_eu_b, w_ed_b, *moe_params)

    return (y_p.reshape(bp, seq, D_MODEL), y_s.reshape(bs, dseq, D_MODEL),
            pool_p[None, :, POOL_HALO - POOL_BUF:, :], conv_p[None, :, CONV_HALO - (CONV_WIDTH - 1):, :],
            h_p[None, :, 0, :],
            pool_s.transpose(1, 0, 2)[None], conv_s.transpose(1, 0, 2)[None], h_s[None])
```

```python
import functools

import jax
import jax.numpy as jnp
from jax import lax
from jax.experimental import pallas as pl
from jax.experimental.pallas import tpu as pltpu

F32 = jnp.float32
BF16 = jnp.bfloat16
I32 = jnp.int32

D_MODEL = 1024
D_POOL = 512
D_RNN = 512
POOL_WINDOWS = (2, 4, 8, 16)
POOL_GROUP = 128
POOL_BUF = 15
N_RNN_HEADS = 8
RNN_HEAD = 64
CONV_WIDTH = 4
LRU_C = 8.0
N_EXPERT_GROUPS = 4
EXPERTS_PER_GROUP = 4
N_EXPERTS = 16
D_EXPERT = 512
D_PLE = 256
LN_EPS = 1e-5
DEPTH = 1
ALPHA = (2 * DEPTH) ** 0.25
PAST_LEN = 16384

LANES = 128
SUBLANES = 8
ROUTER_ROWS = 32
PAIRS = ((0, 1), (0, 2), (0, 3), (1, 2), (1, 3), (2, 3))
N_CLASSES = N_EXPERT_GROUPS * len(PAIRS)
CLASS_ROWS = 32
POOL_HALO = 16
CONV_HALO = 8
PROMPT_TILE = 256
PROMPT_STREAMS = 2
MOE_TILE = 256
UNROLL = SUBLANES
UNROLL_LOG2 = 3
D_PAY = D_MODEL + LANES + D_PLE
VMEM_LIMIT = 56 * 1024 * 1024


def _layer_norm(x, g, b):
    mu = jnp.mean(x, axis=-1, keepdims=True)
    xc = x - mu
    var = jnp.mean(xc * xc, axis=-1, keepdims=True)
    return xc * lax.rsqrt(var + LN_EPS) * g + b


def _bdot(a, w):
    return jnp.dot(a.astype(BF16), w, preferred_element_type=F32)


def _gates(xc, gate_w_ref, ba, bx):
    r_parts, i_parts = [], []
    for j in range(D_RNN // LANES):
        o = _bdot(xc[:, LANES * j:LANES * (j + 1)], gate_w_ref[j])
        r_parts.append(o[:, :LANES])
        i_parts.append(o[:, LANES:])
    r = jax.nn.sigmoid(jnp.concatenate(r_parts, axis=-1) + ba)
    i = jax.nn.sigmoid(jnp.concatenate(i_parts, axis=-1) + bx)
    return r, i


def _lru_terms(xc, r, i, lam, first_pos_mask):
    log_a = -LRU_C * r * jax.nn.softplus(-lam)
    a = jnp.exp(log_a)
    mult = jnp.sqrt(jnp.tanh(-log_a) * (a * a + 1.0))
    if first_pos_mask is not None:
        mult = jnp.where(first_pos_mask, 1.0, mult)
    return a, mult * i * xc


def _pool_project(d, pool_w_ref, pool_b, pool_scale):
    outs = []
    for g in range(len(POOL_WINDOWS)):
        outs.append(_bdot(d[:, POOL_GROUP * g:POOL_GROUP * (g + 1)], pool_w_ref[g]))
    return (jnp.concatenate(outs, axis=-1) + pool_b) * pool_scale


def _route(x1, wr_hi_ref, wr_lo_ref, br):
    hi = x1.astype(BF16)
    lo = (x1 - hi.astype(F32)).astype(BF16)
    logits_tm = (jnp.dot(hi, wr_hi_ref[...], preferred_element_type=F32)
                 + jnp.dot(lo, wr_hi_ref[...], preferred_element_type=F32)
                 + jnp.dot(hi, wr_lo_ref[...], preferred_element_type=F32))
    logits = logits_tm.T[:ROUTER_ROWS, :] + br
    lg = [logits[j:j + 1, :] for j in range(N_EXPERT_GROUPS)]
    m = functools.reduce(jnp.maximum, lg)
    ex = [jnp.exp(v - m) for v in lg]
    den = functools.reduce(lambda p, q: p + q, ex)
    gp = [v / den for v in ex]
    g_w, g_idx = gp[0], jnp.zeros_like(gp[0], dtype=I32)
    for j in range(1, N_EXPERT_GROUPS):
        upd = gp[j] > g_w
        g_idx = jnp.where(upd, j, g_idx)
        g_w = jnp.where(upd, gp[j], g_w)
    e_in = []
    for k in range(EXPERTS_PER_GROUP):
        v = jnp.zeros_like(g_w)
        for g in range(N_EXPERT_GROUPS):
            row = N_EXPERT_GROUPS + g * EXPERTS_PER_GROUP + k
            v = v + jnp.where(g_idx == g, logits[row:row + 1, :], 0.0)
        e_in.append(v)
    m = functools.reduce(jnp.maximum, e_in)
    ex = [jnp.exp(v - m) for v in e_in]
    den = functools.reduce(lambda p, q: p + q, ex)
    ep = [v / den for v in ex]
    w0, i0 = ep[0], jnp.zeros_like(g_idx)
    for k in range(1, EXPERTS_PER_GROUP):
        upd = ep[k] > w0
        i0 = jnp.where(upd, k, i0)
        w0 = jnp.where(upd, ep[k], w0)
    w1, i1 = jnp.full_like(w0, -1.0), jnp.zeros_like(g_idx)
    for k in range(EXPERTS_PER_GROUP):
        upd = (ep[k] > w1) & (i0 != k)
        i1 = jnp.where(upd, k, i1)
        w1 = jnp.where(upd, ep[k], w1)
    wsum = w0 + w1
    c0 = g_w * (w0 / wsum)
    c1 = g_w * (w1 / wsum)
    first_is_lo = i0 < i1
    e_lo = jnp.minimum(i0, i1)
    e_hi = jnp.maximum(i0, i1)
    pair = jnp.zeros_like(g_idx)
    for p, (a, b) in enumerate(PAIRS):
        pair = jnp.where((e_lo == a) & (e_hi == b), p, pair)
    cls = g_idx * len(PAIRS) + pair
    return cls, jnp.where(first_is_lo, c0, c1), jnp.where(first_is_lo, c1, c0)


def _rank_in_class(cls, cnt_ref):
    t = cls.shape[1]
    onehot = (lax.broadcasted_iota(I32, (CLASS_ROWS, t), 0) == cls).astype(F32)
    before = (lax.broadcasted_iota(I32, (t, t), 0) < lax.broadcasted_iota(I32, (t, t), 1)).astype(BF16)
    prefix = jnp.dot(onehot.astype(BF16), before, preferred_element_type=F32)
    rank = jnp.sum(onehot * (prefix + cnt_ref[...]), axis=0, keepdims=True)
    cnt_ref[...] = cnt_ref[...] + jnp.sum(onehot, axis=1, keepdims=True)
    return rank


def _write_token_rows(pay_ref, route_ref, x1, pe, cls, rank, c_lo, c_hi):
    t = x1.shape[0]
    meta_t = jnp.concatenate([c_lo, c_hi, jnp.zeros((LANES - 2, t), F32)], axis=0)
    pay_ref[:, :D_MODEL] = x1
    pay_ref[:, D_MODEL:D_MODEL + LANES] = meta_t.T
    pay_ref[:, D_MODEL + LANES:] = pe
    route_ref[...] = jnp.concatenate([cls.astype(F32), rank, jnp.zeros((SUBLANES - 2, t), F32)], axis=0)


def _scan_rows(a, b, h0):
    t, c = a.shape
    sub = lax.broadcasted_iota(I32, (SUBLANES, c), 0)
    keep = {s: sub >= s for s in (1, 2, 4)}
    h, out = h0, []
    for g in range(t // SUBLANES):
        ag = a[SUBLANES * g:SUBLANES * (g + 1), :]
        bg = b[SUBLANES * g:SUBLANES * (g + 1), :]
        for s in (1, 2, 4):
            a_prev = jnp.where(keep[s], pltpu.roll(ag, s, 0), 1.0)
            b_prev = jnp.where(keep[s], pltpu.roll(bg, s, 0), 0.0)
            bg = ag * b_prev + bg
            ag = ag * a_prev
        hg = ag * h + bg
        h = hg[SUBLANES - 1:SUBLANES, :]
        out.append(hg)
    return jnp.concatenate(out, axis=0)


def _shift_rows(x, s):
    return pltpu.roll(x, s, 0)


def _prompt_kernel(x_ref, pe_ref, ln_in_g, ln_in_b, w_in, pool_w, pool_b, pool_scale, conv_w, conv_b,
                   gate_w, ba, bx, lam, w_out, ln1_g, ln1_b, wr_hi, wr_lo, br,
                   pay_ref, route_ref, cnt_out_ref, pool_new_ref, conv_new_ref, h_new_ref,
                   pool_halo, conv_halo, h_carry, cnt):
    t_idx = pl.program_id(1)

    @pl.when((pl.program_id(0) == 0) & (t_idx == 0))
    def _():
        cnt[...] = jnp.zeros_like(cnt)

    @pl.when(t_idx == 0)
    def _():
        pool_halo[...] = jnp.zeros_like(pool_halo)
        conv_halo[...] = jnp.zeros_like(conv_halo)
        h_carry[...] = jnp.zeros_like(h_carry)

    for s in range(x_ref.shape[0]):
        _prompt_tile(t_idx, x_ref.at[s], pe_ref.at[s], ln_in_g, ln_in_b, w_in, pool_w, pool_b, pool_scale,
                     conv_w, conv_b, gate_w, ba, bx, lam, w_out, ln1_g, ln1_b, wr_hi, wr_lo, br,
                     pay_ref.at[s], route_ref.at[s], pool_new_ref.at[s], conv_new_ref.at[s], h_new_ref.at[s],
                     pool_halo.at[s], conv_halo.at[s], h_carry.at[s], cnt)
    cnt_out_ref[...] = cnt[...]


def _prompt_tile(t_idx, x_ref, pe_ref, ln_in_g, ln_in_b, w_in, pool_w, pool_b, pool_scale, conv_w, conv_b,
                 gate_w, ba, bx, lam, w_out, ln1_g, ln1_b, wr_hi, wr_lo, br,
                 pay_ref, route_ref, pool_new_ref, conv_new_ref, h_new_ref,
                 pool_halo, conv_halo, h_carry, cnt):
    tt = x_ref.shape[0]
    xn = _layer_norm(x_ref[...], ln_in_g[...], ln_in_b[...])
    proj = _bdot(xn, w_in[...])
    u = proj[:, :D_POOL]
    xr = proj[:, D_POOL:D_POOL + D_RNN]
    gate = proj[:, D_POOL + D_RNN:]

    row = lax.broadcasted_iota(I32, (tt, 1), 0)
    pos = t_idx * tt + row

    e = jnp.concatenate([pool_halo[...], u], axis=0)
    s = e + _shift_rows(e, 1)
    sums = [s[:, :POOL_GROUP]]
    s = s[:, POOL_GROUP:]
    for step in (2, 4, 8):
        s = s + _shift_rows(s, step)
        sums.append(s[:, :POOL_GROUP])
        s = s[:, POOL_GROUP:]
    d_parts = []
    for g, win in enumerate(POOL_WINDOWS):
        cnt_w = jnp.minimum(pos + 1, win).astype(F32)
        d_parts.append(sums[g][POOL_HALO:, :] / cnt_w - u[:, POOL_GROUP * g:POOL_GROUP * (g + 1)])
    pool_out = _pool_project(jnp.concatenate(d_parts, axis=-1), pool_w, pool_b[...], pool_scale[...])
    pool_halo[...] = u[tt - POOL_HALO:, :]

    ec = jnp.concatenate([conv_halo[...], xr], axis=0)
    xc = conv_b[...] + _shift_rows(ec, 3) * conv_w[0:1, :]
    xc = xc + _shift_rows(ec, 2) * conv_w[1:2, :]
    xc = xc + _shift_rows(ec, 1) * conv_w[2:3, :]
    xc = (xc + ec * conv_w[3:4, :])[CONV_HALO:, :]
    conv_halo[...] = xr[tt - CONV_HALO:, :]

    r, i = _gates(xc, gate_w, ba[...], bx[...])
    a, b = _lru_terms(xc, r, i, lam[...], pos == 0)
    h = _scan_rows(a, b, h_carry[...])
    h_carry[...] = h[tt - 1:tt, :]
    rnn_out = h * jax.nn.gelu(gate)

    mix = _bdot(jnp.concatenate([pool_out, rnn_out], axis=-1), w_out[...])
    x1 = _layer_norm(ALPHA * xn + mix, ln1_g[...], ln1_b[...])
    cls, c_lo, c_hi = _route(x1, wr_hi, wr_lo, br[...])
    rank = _rank_in_class(cls, cnt)
    _write_token_rows(pay_ref, route_ref, x1, pe_ref[...], cls, rank, c_lo, c_hi)
    pool_new_ref[...] = u[tt - POOL_HALO:, :]
    conv_new_ref[...] = xr[tt - CONV_HALO:, :]
    h_new_ref[...] = jnp.broadcast_to(h[tt - 1:tt, :], h_new_ref.shape)


def _sample_kernel(x_ref, pe_ref, pool_buf_ref, conv_buf_ref, h0_ref, cnt_in_ref,
                   ln_in_g, ln_in_b, w_in, pool_w, pool_b, pool_scale, conv_w, conv_b,
                   gate_w, ba, bx, lam, w_out, ln1_g, ln1_b, wr_hi, wr_lo, br,
                   pay_ref, route_ref, cnt_out_ref, pool_new_ref, conv_new_ref, h_new_ref):
    nb = h0_ref.shape[0]
    t_len = x_ref.shape[0] // nb
    xn = _layer_norm(x_ref[...], ln_in_g[...], ln_in_b[...])
    proj = _bdot(xn, w_in[...])
    u = proj[:, :D_POOL]
    xr = proj[:, D_POOL:D_POOL + D_RNN]
    gate = proj[:, D_POOL + D_RNN:]

    f = [pool_buf_ref[k] for k in range(POOL_BUF)] + [u[nb * t:nb * (t + 1), :] for t in range(t_len)]
    n_rows = len(f)
    s = [f[0]] + [f[k] + f[k - 1] for k in range(1, n_rows)]
    sums = [[v[:, :POOL_GROUP] for v in s]]
    s = [v[:, POOL_GROUP:] for v in s]
    for step in (2, 4, 8):
        s = [s[k] + s[k - step] if k >= step else s[k] for k in range(n_rows)]
        sums.append([v[:, :POOL_GROUP] for v in s])
        s = [v[:, POOL_GROUP:] for v in s]
    d_rows = []
    for t in range(t_len):
        parts = []
        for g, win in enumerate(POOL_WINDOWS):
            cnt_w = float(min(PAST_LEN + t + 1, win))
            parts.append(sums[g][POOL_BUF + t] / cnt_w - f[POOL_BUF + t][:, POOL_GROUP * g:POOL_GROUP * (g + 1)])
        d_rows.append(jnp.concatenate(parts, axis=-1))
    pool_out = _pool_project(jnp.concatenate(d_rows, axis=0), pool_w, pool_b[...], pool_scale[...])
    for k in range(POOL_BUF):
        pool_new_ref[k] = f[n_rows - POOL_BUF + k]

    gbuf = [conv_buf_ref[k] for k in range(CONV_WIDTH - 1)] + [xr[nb * t:nb * (t + 1), :] for t in range(t_len)]
    xc_rows = []
    for t in range(t_len):
        v = conv_b[...] + gbuf[t] * conv_w[0:1, :]
        for k in range(1, CONV_WIDTH):
            v = v + gbuf[t + k] * conv_w[k:k + 1, :]
        xc_rows.append(v)
    xc = jnp.concatenate(xc_rows, axis=0)
    for k in range(CONV_WIDTH - 1):
        conv_new_ref[k] = gbuf[len(gbuf) - (CONV_WIDTH - 1) + k]

    r, i = _gates(xc, gate_w, ba[...], bx[...])
    a, b = _lru_terms(xc, r, i, lam[...], None)
    h = h0_ref[...]
    h_rows = []
    for t in range(t_len):
        h = a[nb * t:nb * (t + 1), :] * h + b[nb * t:nb * (t + 1), :]
        h_rows.append(h)
    h_new_ref[...] = h
    rnn_out = jnp.concatenate(h_rows, axis=0) * jax.nn.gelu(gate)

    mix = _bdot(jnp.concatenate([pool_out, rnn_out], axis=-1), w_out[...])
    x1 = _layer_norm(ALPHA * xn + mix, ln1_g[...], ln1_b[...])
    cls, c_lo, c_hi = _route(x1, wr_hi, wr_lo, br[...])
    cnt_out_ref[...] = cnt_in_ref[...]
    rank = _rank_in_class(cls, cnt_out_ref)
    _write_token_rows(pay_ref, route_ref, x1, pe_ref[...], cls, rank, c_lo, c_hi)


def _moe_kernel(dest_ref, nv_ref, nvp_ref, ea_ref, eb_ref,
                payp_hbm, pays_hbm, wg_a, wu_a, wd_a, wg_b, wu_b, wd_b,
                ln2_g, ln2_b, w_pg, b_pg, w_ple, ln3_g, ln3_b,
                yp_hbm, ys_hbm, trash_hbm,
                src, gbuf, obuf, gsem, ssem, *, sample_batch, sample_seq):
    del ea_ref, eb_ref
    t = pl.program_id(0)
    n_tiles = pl.num_programs(0)
    tm = gbuf.shape[1] * SUBLANES
    slot = t % 2
    n_tok = dest_ref.shape[0]
    n_prompt = payp_hbm.shape[0]
    log2_batch = sample_batch.bit_length() - 1

    def rows8(tile):
        return pl.multiple_of((nv_ref[tile] + (SUBLANES - 1)) & -SUBLANES, SUBLANES)

    def buf_row(buf, slot_, j):
        return buf.at[slot_, lax.shift_right_logical(j, UNROLL_LOG2), pl.ds(j & (SUBLANES - 1), 1), :]

    def for_each_row(lo, hi, per_row):
        def row_body(j, c):
            per_row(j)
            return c
        lax.fori_loop(lo, hi, row_body, 0)

    def for_rows(hi, per_group, per_row):
        n_groups = lax.shift_right_logical(hi, UNROLL_LOG2)

        def group_body(g, c):
            per_group(g)
            return c
        lax.fori_loop(0, n_groups, group_body, 0)
        for_each_row(lax.shift_left(n_groups, UNROLL_LOG2), hi, per_row)

    def issue_gather(tile, slot_):
        base = tile * tm

        def from_prompt_group(g):
            for k in range(UNROLL):
                pltpu.make_async_copy(payp_hbm.at[pl.ds(src[base + g * UNROLL + k], 1), :],
                                      gbuf.at[slot_, g, pl.ds(k, 1), :], gsem.at[slot_]).start()

        def from_prompt(j):
            pltpu.make_async_copy(payp_hbm.at[pl.ds(src[base + j], 1), :], buf_row(gbuf, slot_, j),
                                  gsem.at[slot_]).start()

        def from_sample(j):
            pltpu.make_async_copy(pays_hbm.at[pl.ds(src[base + j] - n_prompt, 1), :], buf_row(gbuf, slot_, j),
                                  gsem.at[slot_]).start()

        def rounding(j):
            pltpu.make_async_copy(payp_hbm.at[pl.ds(0, 1), :], buf_row(gbuf, slot_, j), gsem.at[slot_]).start()

        nvp_, nv_ = nvp_ref[tile], nv_ref[tile]
        for_rows(nvp_, from_prompt_group, from_prompt)
        for_each_row(nvp_, nv_, from_sample)
        for_each_row(nv_, rows8(tile), rounding)

    def wait_gather(tile, slot_):
        n = rows8(tile)

        @pl.when(n > 0)
        def _():
            rows = payp_hbm.at[pl.ds(0, n), :]
            pltpu.make_async_copy(rows, rows, gsem.at[slot_]).wait()

    def wait_scatter(tile, slot_):
        n = rows8(tile)

        @pl.when(n > 0)
        def _():
            rows = yp_hbm.at[pl.ds(0, n), :]
            pltpu.make_async_copy(rows, rows, ssem.at[slot_]).wait()

    @pl.when(t == 0)
    def _():
        gbuf[...] = jnp.zeros_like(gbuf)
        init = pltpu.make_async_copy(gbuf.at[0, pl.ds(0, 2), :, pl.ds(0, D_MODEL)], trash_hbm, ssem.at[0])
        init.start()
        init.wait()

        def fill(g, c):
            for k in range(UNROLL):
                i = g * UNROLL + k
                src[dest_ref[i]] = i
            return c
        lax.fori_loop(0, n_tok // UNROLL, fill, 0)
        issue_gather(0, 0)

    @pl.when(t + 1 < n_tiles)
    def _():
        issue_gather(t + 1, 1 - slot)

    wait_gather(t, slot)

    @pl.when(t >= 2)
    def _():
        wait_scatter(t - 2, slot)

    nv = nv_ref[t]

    @pl.when(nv > 0)
    def _():
        xp = jnp.concatenate([gbuf[slot, g] for g in range(tm // SUBLANES)], axis=0)
        x1 = xp[:, :D_MODEL]
        meta = xp[:, D_MODEL:D_MODEL + LANES]
        pe = xp[:, D_MODEL + LANES:]
        xb = x1.astype(BF16)

        def expert(wg, wu, wd):
            hid = jax.nn.silu(jnp.dot(xb, wg[...], preferred_element_type=F32)) * jnp.dot(
                xb, wu[...], preferred_element_type=F32)
            return _bdot(hid, wd[...])

        moe = meta[:, 0:1] * expert(wg_a, wu_a, wd_a)
        moe = moe + meta[:, 1:2] * expert(wg_b, wu_b, wd_b)
        x2 = _layer_norm(ALPHA * x1 + moe, ln2_g[...], ln2_b[...])
        gate_p = jax.nn.sigmoid(_bdot(x2, w_pg[...]) + b_pg[...])
        ple = _bdot(pe, w_ple[...]) * gate_p
        y = _layer_norm(ALPHA * x2 + ple, ln3_g[...], ln3_b[...])
        for g in range(tm // SUBLANES):
            obuf[slot, g] = y[SUBLANES * g:SUBLANES * (g + 1), :]

    base = t * tm
    nvp = nvp_ref[t]

    def to_prompt_group(g):
        for k in range(UNROLL):
            pltpu.make_async_copy(obuf.at[slot, g, pl.ds(k, 1), :],
                                  yp_hbm.at[pl.ds(src[base + g * UNROLL + k], 1), :], ssem.at[slot]).start()

    def to_prompt(j):
        pltpu.make_async_copy(buf_row(obuf, slot, j), yp_hbm.at[pl.ds(src[base + j], 1), :], ssem.at[slot]).start()

    def to_sample(j):
        s = src[base + j] - n_prompt
        r = (s & (sample_batch - 1)) * sample_seq + lax.shift_right_logical(s, log2_batch)
        pltpu.make_async_copy(buf_row(obuf, slot, j), ys_hbm.at[pl.ds(r, 1), :], ssem.at[slot]).start()

    def to_trash(j):
        pltpu.make_async_copy(buf_row(obuf, slot, j), trash_hbm.at[slot, pl.ds(j - nv, 1), :], ssem.at[slot]).start()

    for_rows(nvp, to_prompt_group, to_prompt)
    for_each_row(nvp, nv, to_sample)
    for_each_row(nv, rows8(t), to_trash)

    @pl.when(t == n_tiles - 1)
    def _():
        @pl.when(t >= 1)
        def _():
            wait_scatter(t - 1, 1 - slot)
        wait_scatter(t, slot)


def _full(shape):
    nd = len(shape)
    return pl.BlockSpec(shape, lambda *_: (0,) * nd)


def _row(v):
    return v.reshape(1, -1).astype(F32)


def kernel(x_prompt, x_sample, state_pool, state_conv, state_h, p_prompt, p_sample, ln_in_g, ln_in_b, w_in,
           pool_w, pool_b, pool_scale, conv_w, conv_b, rg_wa, rg_ba, rg_wx, rg_bx, rg_lambda, w_out, ln1_g,
           ln1_b, w_rg, b_rg, w_re, b_re, w_eg, w_eu, w_ed, ln2_g, ln2_b, w_pg, b_pg, w_ple, ln3_g, ln3_b):
    bp, seq, _ = x_prompt.shape
    bs, dseq, _ = x_sample.shape
    n_p, n_s = bp * seq, bs * dseq
    n_tok = n_p + n_s
    assert seq % PROMPT_TILE == 0 and bp % PROMPT_STREAMS == 0 and n_tok % MOE_TILE == 0
    assert bs & (bs - 1) == 0, "sample batch must be a power of two"

    w_in_b = w_in[0].astype(BF16)
    w_out_b = w_out[0].astype(BF16)
    pool_w_b = pool_w[0].astype(BF16)
    zero = jnp.zeros((RNN_HEAD, RNN_HEAD), F32)

    def pair_block(w, j):
        top = jnp.concatenate([w[2 * j], zero], axis=1)
        bot = jnp.concatenate([zero, w[2 * j + 1]], axis=1)
        return jnp.concatenate([top, bot], axis=0)

    gate_w = jnp.stack([jnp.concatenate([pair_block(rg_wa[0], j), pair_block(rg_wx[0], j)], axis=1)
                        for j in range(N_RNN_HEADS // 2)]).astype(BF16)
    wr = jnp.concatenate([w_rg[0], w_re[0],
                          jnp.zeros((D_MODEL, LANES - N_EXPERT_GROUPS - N_EXPERTS), F32)], axis=1)
    wr_hi = wr.astype(BF16)
    wr_lo = (wr - wr_hi.astype(F32)).astype(BF16)
    br = jnp.concatenate([b_rg[0], b_re[0],
                          jnp.zeros((ROUTER_ROWS - N_EXPERT_GROUPS - N_EXPERTS,), F32)]).reshape(ROUTER_ROWS, 1)

    mixer_params = (_row(ln_in_g), _row(ln_in_b), w_in_b, pool_w_b, _row(pool_b[0]), _row(pool_scale[0]),
                    conv_w[0], _row(conv_b[0]), gate_w, _row(rg_ba[0]), _row(rg_bx[0]), _row(rg_lambda[0]),
                    w_out_b, _row(ln1_g[0]), _row(ln1_b[0]), wr_hi, wr_lo, br)
    mixer_specs = [_full(p.shape) for p in mixer_params]

    n_t = seq // PROMPT_TILE
    nb = PROMPT_STREAMS
    pay_p, route_p, cnt_p, pool_p, conv_p, h_p = pl.pallas_call(
        _prompt_kernel,
        grid=(bp // nb, n_t),
        in_specs=[pl.BlockSpec((nb, PROMPT_TILE, D_MODEL), lambda b, t: (b, t, 0)),
                  pl.BlockSpec((None, nb, PROMPT_TILE, D_PLE), lambda b, t: (0, b, t, 0))] + mixer_specs,
        out_specs=[
            pl.BlockSpec((nb, PROMPT_TILE, D_PAY), lambda b, t: (b, t, 0)),
            pl.BlockSpec((nb, SUBLANES, PROMPT_TILE), lambda b, t: (b, 0, t)),
            _full((CLASS_ROWS, 1)),
            pl.BlockSpec((nb, POOL_HALO, D_POOL), lambda b, t: (b, 0, 0)),
            pl.BlockSpec((nb, CONV_HALO, D_RNN), lambda b, t: (b, 0, 0)),
            pl.BlockSpec((nb, SUBLANES, D_RNN), lambda b, t: (b, 0, 0)),
        ],
        out_shape=[
            jax.ShapeDtypeStruct((bp, seq, D_PAY), F32),
            jax.ShapeDtypeStruct((bp, SUBLANES, seq), F32),
            jax.ShapeDtypeStruct((CLASS_ROWS, 1), F32),
            jax.ShapeDtypeStruct((bp, POOL_HALO, D_POOL), F32),
            jax.ShapeDtypeStruct((bp, CONV_HALO, D_RNN), F32),
            jax.ShapeDtypeStruct((bp, SUBLANES, D_RNN), F32),
        ],
        scratch_shapes=[pltpu.VMEM((nb, POOL_HALO, D_POOL), F32), pltpu.VMEM((nb, CONV_HALO, D_RNN), F32),
                        pltpu.VMEM((nb, 1, D_RNN), F32), pltpu.VMEM((CLASS_ROWS, 1), F32)],
        compiler_params=pltpu.CompilerParams(dimension_semantics=("arbitrary", "arbitrary"),
                                             vmem_limit_bytes=VMEM_LIMIT),
        name="prompt_mixers",
    )(x_prompt, p_prompt, *mixer_params)
    pay_p = pay_p.reshape(n_p, D_PAY)

    xs_tm = x_sample.transpose(1, 0, 2).reshape(n_s, D_MODEL)
    pe_tm = p_sample[0].transpose(1, 0, 2).reshape(n_s, D_PLE)
    pool_buf_tm = state_pool[0].transpose(1, 0, 2)
    conv_buf_tm = state_conv[0].transpose(1, 0, 2)
    sample_in = (xs_tm, pe_tm, pool_buf_tm, conv_buf_tm, state_h[0], cnt_p)
    pay_s, route_s, cnt_all, pool_s, conv_s, h_s = pl.pallas_call(
        _sample_kernel,
        grid=(1,),
        in_specs=[_full(p.shape) for p in sample_in] + mixer_specs,
        out_specs=[_full((n_s, D_PAY)), _full((SUBLANES, n_s)),
                   _full((CLASS_ROWS, 1)), _full((POOL_BUF, bs, D_POOL)),
                   _full((CONV_WIDTH - 1, bs, D_RNN)), _full((bs, D_RNN))],
        out_shape=[
            jax.ShapeDtypeStruct((n_s, D_PAY), F32),
            jax.ShapeDtypeStruct((SUBLANES, n_s), F32),
            jax.ShapeDtypeStruct((CLASS_ROWS, 1), F32),
            jax.ShapeDtypeStruct((POOL_BUF, bs, D_POOL), F32),
            jax.ShapeDtypeStruct((CONV_WIDTH - 1, bs, D_RNN), F32),
            jax.ShapeDtypeStruct((bs, D_RNN), F32),
        ],
        compiler_params=pltpu.CompilerParams(dimension_semantics=("arbitrary",), vmem_limit_bytes=VMEM_LIMIT),
        name="sample_mixers",
    )(*sample_in, *mixer_params)

    tm = MOE_TILE
    n_tiles = n_tok // tm + N_CLASSES
    counts = cnt_all[:N_CLASSES, 0].astype(I32)
    counts_p = cnt_p[:N_CLASSES, 0].astype(I32)
    tiles_c = (counts + tm - 1) // tm
    tile_end = jnp.cumsum(tiles_c)
    tile_start = tile_end - tiles_c
    row_start = tile_start * tm
    cls_all = jnp.concatenate([route_p[:, 0, :].reshape(n_p), route_s[0]]).astype(I32)
    rank_all = jnp.concatenate([route_p[:, 1, :].reshape(n_p), route_s[1]]).astype(I32)
    dest = row_start[cls_all] + rank_all
    tile_id = jnp.arange(n_tiles, dtype=I32)
    tile_cls = jnp.minimum(jnp.sum((tile_id[:, None] >= tile_end[None, :]).astype(I32), axis=1), N_CLASSES - 1)
    used = tile_id < tile_end[-1]
    first_row = (tile_id - tile_start[tile_cls]) * tm
    nv = jnp.where(used, jnp.clip(counts[tile_cls] - first_row, 0, tm), 0).astype(I32)
    nvp = jnp.minimum(jnp.clip(counts_p[tile_cls] - first_row, 0, tm), nv).astype(I32)
    pair_lo = jnp.array([p[0] for p in PAIRS], I32)
    pair_hi = jnp.array([p[1] for p in PAIRS], I32)
    tile_group = tile_cls // len(PAIRS)
    ea = (tile_group * EXPERTS_PER_GROUP + pair_lo[tile_cls % len(PAIRS)]).astype(I32)
    eb = (tile_group * EXPERTS_PER_GROUP + pair_hi[tile_cls % len(PAIRS)]).astype(I32)

    moe_params = (_row(ln2_g[0]), _row(ln2_b[0]), w_pg[0].astype(BF16), _row(b_pg[0]), w_ple[0].astype(BF16),
                  _row(ln3_g[0]), _row(ln3_b[0]))
    w_eg_b, w_eu_b, w_ed_b = w_eg[0].astype(BF16), w_eu[0].astype(BF16), w_ed[0].astype(BF16)
    up_spec_a = pl.BlockSpec((None, D_MODEL, D_EXPERT), lambda t, d, n, p, a, b: (a[t], 0, 0))
    up_spec_b = pl.BlockSpec((None, D_MODEL, D_EXPERT), lambda t, d, n, p, a, b: (b[t], 0, 0))
    dn_spec_a = pl.BlockSpec((None, D_EXPERT, D_MODEL), lambda t, d, n, p, a, b: (a[t], 0, 0))
    dn_spec_b = pl.BlockSpec((None, D_EXPERT, D_MODEL), lambda t, d, n, p, a, b: (b[t], 0, 0))
    y_p, y_s, _ = pl.pallas_call(
        functools.partial(_moe_kernel, sample_batch=bs, sample_seq=dseq),
        grid_spec=pltpu.PrefetchScalarGridSpec(
            num_scalar_prefetch=5,
            grid=(n_tiles,),
            in_specs=[pl.BlockSpec(memory_space=pl.ANY), pl.BlockSpec(memory_space=pl.ANY),
                      up_spec_a, up_spec_a, dn_spec_a,
                      up_spec_b, up_spec_b, dn_spec_b] + [_full(p.shape) for p in moe_params],
            out_specs=[pl.BlockSpec(memory_space=pl.ANY)] * 3,
            scratch_shapes=[pltpu.SMEM((n_tiles * tm,), I32),
                            pltpu.VMEM((2, tm // SUBLANES, SUBLANES, D_PAY), F32),
                            pltpu.VMEM((2, tm // SUBLANES, SUBLANES, D_MODEL), F32),
                            pltpu.SemaphoreType.DMA((2,)), pltpu.SemaphoreType.DMA((2,))],
        ),
        out_shape=[jax.ShapeDtypeStruct((n_p, D_MODEL), F32), jax.ShapeDtypeStruct((n_s, D_MODEL), F32),
                   jax.ShapeDtypeStruct((2, SUBLANES, D_MODEL), F32)],
        compiler_params=pltpu.CompilerParams(dimension_semantics=("arbitrary",), vmem_limit_bytes=VMEM_LIMIT),
        name="moe_embed",
    )(dest, nv, nvp, ea, eb, pay_p, pay_s, w_eg_b, w_eu_b, w_ed_b, w_eg_b, w_eu_b, w_ed_b, *moe_params)

    return (y_p.reshape(bp, seq, D_MODEL), y_s.reshape(bs, dseq, D_MODEL),
            pool_p[None, :, POOL_HALO - POOL_BUF:, :], conv_p[None, :, CONV_HALO - (CONV_WIDTH - 1):, :],
            h_p[None, :, 0, :],
            pool_s.transpose(1, 0, 2)[None], conv_s.transpose(1, 0, 2)[None], h_s[None])
```

```python
import functools

import jax
import jax.numpy as jnp
from jax import lax
from jax.experimental import pallas as pl
from jax.experimental.pallas import tpu as pltpu

F32 = jnp.float32
BF16 = jnp.bfloat16
I32 = jnp.int32

D_MODEL = 1024
D_POOL = 512
D_RNN = 512
POOL_WINDOWS = (2, 4, 8, 16)
POOL_GROUP = 128
POOL_BUF = 15
N_RNN_HEADS = 8
RNN_HEAD = 64
CONV_WIDTH = 4
LRU_C = 8.0
N_EXPERT_GROUPS = 4
EXPERTS_PER_GROUP = 4
N_EXPERTS = 16
D_EXPERT = 512
D_PLE = 256
LN_EPS = 1e-5
DEPTH = 1
ALPHA = (2 * DEPTH) ** 0.25
PAST_LEN = 16384

LANES = 128
SUBLANES = 8
ROUTER_ROWS = 32
PAIRS = ((0, 1), (0, 2), (0, 3), (1, 2), (1, 3), (2, 3))
N_CLASSES = N_EXPERT_GROUPS * len(PAIRS)
CLASS_ROWS = 32
POOL_HALO = 16
CONV_HALO = 8
PROMPT_TILE = 256
PROMPT_STREAMS = 2
MOE_TILE = 384
UNROLL = SUBLANES
UNROLL_LOG2 = 3
D_PAY = D_MODEL + LANES + D_PLE
VMEM_LIMIT = 56 * 1024 * 1024


def _layer_norm(x, g, b):
    mu = jnp.mean(x, axis=-1, keepdims=True)
    xc = x - mu
    var = jnp.mean(xc * xc, axis=-1, keepdims=True)
    return xc * lax.rsqrt(var + LN_EPS) * g + b


def _bdot(a, w):
    return jnp.dot(a.astype(BF16), w, preferred_element_type=F32)


def _gates(xc, gate_w_ref, ba, bx):
    r_parts, i_parts = [], []
    for j in range(D_RNN // LANES):
        o = _bdot(xc[:, LANES * j:LANES * (j + 1)], gate_w_ref[j])
        r_parts.append(o[:, :LANES])
        i_parts.append(o[:, LANES:])
    r = jax.nn.sigmoid(jnp.concatenate(r_parts, axis=-1) + ba)
    i = jax.nn.sigmoid(jnp.concatenate(i_parts, axis=-1) + bx)
    return r, i


def _lru_terms(xc, r, i, lam, first_pos_mask):
    log_a = -LRU_C * r * jax.nn.softplus(-lam)
    a = jnp.exp(log_a)
    mult = jnp.sqrt(jnp.tanh(-log_a) * (a * a + 1.0))
    if first_pos_mask is not None:
        mult = jnp.where(first_pos_mask, 1.0, mult)
    return a, mult * i * xc


def _pool_project(d, pool_w_ref, pool_b, pool_scale):
    outs = []
    for g in range(len(POOL_WINDOWS)):
        outs.append(_bdot(d[:, POOL_GROUP * g:POOL_GROUP * (g + 1)], pool_w_ref[g]))
    return (jnp.concatenate(outs, axis=-1) + pool_b) * pool_scale


def _route(x1, wr_hi_ref, wr_lo_ref, br):
    hi = x1.astype(BF16)
    lo = (x1 - hi.astype(F32)).astype(BF16)
    logits_tm = (jnp.dot(hi, wr_hi_ref[...], preferred_element_type=F32)
                 + jnp.dot(lo, wr_hi_ref[...], preferred_element_type=F32)
                 + jnp.dot(hi, wr_lo_ref[...], preferred_element_type=F32))
    logits = logits_tm.T[:ROUTER_ROWS, :] + br
    lg = [logits[j:j + 1, :] for j in range(N_EXPERT_GROUPS)]
    m = functools.reduce(jnp.maximum, lg)
    ex = [jnp.exp(v - m) for v in lg]
    den = functools.reduce(lambda p, q: p + q, ex)
    gp = [v / den for v in ex]
    g_w, g_idx = gp[0], jnp.zeros_like(gp[0], dtype=I32)
    for j in range(1, N_EXPERT_GROUPS):
        upd = gp[j] > g_w
        g_idx = jnp.where(upd, j, g_idx)
        g_w = jnp.where(upd, gp[j], g_w)
    e_in = []
    for k in range(EXPERTS_PER_GROUP):
        v = jnp.zeros_like(g_w)
        for g in range(N_EXPERT_GROUPS):
            row = N_EXPERT_GROUPS + g * EXPERTS_PER_GROUP + k
            v = v + jnp.where(g_idx == g, logits[row:row + 1, :], 0.0)
        e_in.append(v)
    m = functools.reduce(jnp.maximum, e_in)
    ex = [jnp.exp(v - m) for v in e_in]
    den = functools.reduce(lambda p, q: p + q, ex)
    ep = [v / den for v in ex]
    w0, i0 = ep[0], jnp.zeros_like(g_idx)
    for k in range(1, EXPERTS_PER_GROUP):
        upd = ep[k] > w0
        i0 = jnp.where(upd, k, i0)
        w0 = jnp.where(upd, ep[k], w0)
    w1, i1 = jnp.full_like(w0, -1.0), jnp.zeros_like(g_idx)
    for k in range(EXPERTS_PER_GROUP):
        upd = (ep[k] > w1) & (i0 != k)
        i1 = jnp.where(upd, k, i1)
        w1 = jnp.where(upd, ep[k], w1)
    wsum = w0 + w1
    c0 = g_w * (w0 / wsum)
    c1 = g_w * (w1 / wsum)
    first_is_lo = i0 < i1
    e_lo = jnp.minimum(i0, i1)
    e_hi = jnp.maximum(i0, i1)
    pair = jnp.zeros_like(g_idx)
    for p, (a, b) in enumerate(PAIRS):
        pair = jnp.where((e_lo == a) & (e_hi == b), p, pair)
    cls = g_idx * len(PAIRS) + pair
    return cls, jnp.where(first_is_lo, c0, c1), jnp.where(first_is_lo, c1, c0)


def _rank_in_class(cls, cnt_ref):
    t = cls.shape[1]
    onehot = (lax.broadcasted_iota(I32, (CLASS_ROWS, t), 0) == cls).astype(F32)
    before = (lax.broadcasted_iota(I32, (t, t), 0) < lax.broadcasted_iota(I32, (t, t), 1)).astype(BF16)
    prefix = jnp.dot(onehot.astype(BF16), before, preferred_element_type=F32)
    rank = jnp.sum(onehot * (prefix + cnt_ref[...]), axis=0, keepdims=True)
    cnt_ref[...] = cnt_ref[...] + jnp.sum(onehot, axis=1, keepdims=True)
    return rank


def _write_token_rows(pay_ref, route_ref, x1, pe, cls, rank, c_lo, c_hi):
    t = x1.shape[0]
    meta_t = jnp.concatenate([c_lo, c_hi, jnp.zeros((LANES - 2, t), F32)], axis=0)
    pay_ref[:, :D_MODEL] = x1
    pay_ref[:, D_MODEL:D_MODEL + LANES] = meta_t.T
    pay_ref[:, D_MODEL + LANES:] = pe
    route_ref[...] = jnp.concatenate([cls.astype(F32), rank, jnp.zeros((SUBLANES - 2, t), F32)], axis=0)


def _scan_rows(a, b, h0):
    t, c = a.shape
    sub = lax.broadcasted_iota(I32, (SUBLANES, c), 0)
    keep = {s: sub >= s for s in (1, 2, 4)}
    h, out = h0, []
    for g in range(t // SUBLANES):
        ag = a[SUBLANES * g:SUBLANES * (g + 1), :]
        bg = b[SUBLANES * g:SUBLANES * (g + 1), :]
        for s in (1, 2, 4):
            a_prev = jnp.where(keep[s], pltpu.roll(ag, s, 0), 1.0)
            b_prev = jnp.where(keep[s], pltpu.roll(bg, s, 0), 0.0)
            bg = ag * b_prev + bg
            ag = ag * a_prev
        hg = ag * h + bg
        h = hg[SUBLANES - 1:SUBLANES, :]
        out.append(hg)
    return jnp.concatenate(out, axis=0)


def _shift_rows(x, s):
    return pltpu.roll(x, s, 0)


def _prompt_kernel(x_ref, pe_ref, ln_in_g, ln_in_b, w_in, pool_w, pool_b, pool_scale, conv_w, conv_b,
                   gate_w, ba, bx, lam, w_out, ln1_g, ln1_b, wr_hi, wr_lo, br,
                   pay_ref, route_ref, cnt_out_ref, pool_new_ref, conv_new_ref, h_new_ref,
                   pool_halo, conv_halo, h_carry, cnt):
    t_idx = pl.program_id(1)

    @pl.when((pl.program_id(0) == 0) & (t_idx == 0))
    def _():
        cnt[...] = jnp.zeros_like(cnt)

    @pl.when(t_idx == 0)
    def _():
        pool_halo[...] = jnp.zeros_like(pool_halo)
        conv_halo[...] = jnp.zeros_like(conv_halo)
        h_carry[...] = jnp.zeros_like(h_carry)

    for s in range(x_ref.shape[0]):
        _prompt_tile(t_idx, x_ref.at[s], pe_ref.at[s], ln_in_g, ln_in_b, w_in, pool_w, pool_b, pool_scale,
                     conv_w, conv_b, gate_w, ba, bx, lam, w_out, ln1_g, ln1_b, wr_hi, wr_lo, br,
                     pay_ref.at[s], route_ref.at[s], pool_new_ref.at[s], conv_new_ref.at[s], h_new_ref.at[s],
                     pool_halo.at[s], conv_halo.at[s], h_carry.at[s], cnt)
    cnt_out_ref[...] = cnt[...]


def _prompt_tile(t_idx, x_ref, pe_ref, ln_in_g, ln_in_b, w_in, pool_w, pool_b, pool_scale, conv_w, conv_b,
                 gate_w, ba, bx, lam, w_out, ln1_g, ln1_b, wr_hi, wr_lo, br,
                 pay_ref, route_ref, pool_new_ref, conv_new_ref, h_new_ref,
                 pool_halo, conv_halo, h_carry, cnt):
    tt = x_ref.shape[0]
    xn = _layer_norm(x_ref[...], ln_in_g[...], ln_in_b[...])
    proj = _bdot(xn, w_in[...])
    u = proj[:, :D_POOL]
    xr = proj[:, D_POOL:D_POOL + D_RNN]
    gate = proj[:, D_POOL + D_RNN:]

    row = lax.broadcasted_iota(I32, (tt, 1), 0)
    pos = t_idx * tt + row

    e = jnp.concatenate([pool_halo[...], u], axis=0)
    s = e + _shift_rows(e, 1)
    sums = [s[:, :POOL_GROUP]]
    s = s[:, POOL_GROUP:]
    for step in (2, 4, 8):
        s = s + _shift_rows(s, step)
        sums.append(s[:, :POOL_GROUP])
        s = s[:, POOL_GROUP:]
    d_parts = []
    for g, win in enumerate(POOL_WINDOWS):
        cnt_w = jnp.minimum(pos + 1, win).astype(F32)
        d_parts.append(sums[g][POOL_HALO:, :] / cnt_w - u[:, POOL_GROUP * g:POOL_GROUP * (g + 1)])
    pool_out = _pool_project(jnp.concatenate(d_parts, axis=-1), pool_w, pool_b[...], pool_scale[...])
    pool_halo[...] = u[tt - POOL_HALO:, :]

    ec = jnp.concatenate([conv_halo[...], xr], axis=0)
    xc = conv_b[...] + _shift_rows(ec, 3) * conv_w[0:1, :]
    xc = xc + _shift_rows(ec, 2) * conv_w[1:2, :]
    xc = xc + _shift_rows(ec, 1) * conv_w[2:3, :]
    xc = (xc + ec * conv_w[3:4, :])[CONV_HALO:, :]
    conv_halo[...] = xr[tt - CONV_HALO:, :]

    r, i = _gates(xc, gate_w, ba[...], bx[...])
    a, b = _lru_terms(xc, r, i, lam[...], pos == 0)
    h = _scan_rows(a, b, h_carry[...])
    h_carry[...] = h[tt - 1:tt, :]
    rnn_out = h * jax.nn.gelu(gate)

    mix = _bdot(jnp.concatenate([pool_out, rnn_out], axis=-1), w_out[...])
    x1 = _layer_norm(ALPHA * xn + mix, ln1_g[...], ln1_b[...])
    cls, c_lo, c_hi = _route(x1, wr_hi, wr_lo, br[...])
    rank = _rank_in_class(cls, cnt)
    _write_token_rows(pay_ref, route_ref, x1, pe_ref[...], cls, rank, c_lo, c_hi)
    pool_new_ref[...] = u[tt - POOL_HALO:, :]
    conv_new_ref[...] = xr[tt - CONV_HALO:, :]
    h_new_ref[...] = jnp.broadcast_to(h[tt - 1:tt, :], h_new_ref.shape)


def _sample_kernel(x_ref, pe_ref, pool_buf_ref, conv_buf_ref, h0_ref, cnt_in_ref,
                   ln_in_g, ln_in_b, w_in, pool_w, pool_b, pool_scale, conv_w, conv_b,
                   gate_w, ba, bx, lam, w_out, ln1_g, ln1_b, wr_hi, wr_lo, br,
                   pay_ref, route_ref, cnt_out_ref, pool_new_ref, conv_new_ref, h_new_ref):
    nb = h0_ref.shape[0]
    t_len = x_ref.shape[0] // nb
    xn = _layer_norm(x_ref[...], ln_in_g[...], ln_in_b[...])
    proj = _bdot(xn, w_in[...])
    u = proj[:, :D_POOL]
    xr = proj[:, D_POOL:D_POOL + D_RNN]
    gate = proj[:, D_POOL + D_RNN:]

    f = [pool_buf_ref[k] for k in range(POOL_BUF)] + [u[nb * t:nb * (t + 1), :] for t in range(t_len)]
    n_rows = len(f)
    s = [f[0]] + [f[k] + f[k - 1] for k in range(1, n_rows)]
    sums = [[v[:, :POOL_GROUP] for v in s]]
    s = [v[:, POOL_GROUP:] for v in s]
    for step in (2, 4, 8):
        s = [s[k] + s[k - step] if k >= step else s[k] for k in range(n_rows)]
        sums.append([v[:, :POOL_GROUP] for v in s])
        s = [v[:, POOL_GROUP:] for v in s]
    d_rows = []
    for t in range(t_len):
        parts = []
        for g, win in enumerate(POOL_WINDOWS):
            cnt_w = float(min(PAST_LEN + t + 1, win))
            parts.append(sums[g][POOL_BUF + t] / cnt_w - f[POOL_BUF + t][:, POOL_GROUP * g:POOL_GROUP * (g + 1)])
        d_rows.append(jnp.concatenate(parts, axis=-1))
    pool_out = _pool_project(jnp.concatenate(d_rows, axis=0), pool_w, pool_b[...], pool_scale[...])
    for k in range(POOL_BUF):
        pool_new_ref[k] = f[n_rows - POOL_BUF + k]

    gbuf = [conv_buf_ref[k] for k in range(CONV_WIDTH - 1)] + [xr[nb * t:nb * (t + 1), :] for t in range(t_len)]
    xc_rows = []
    for t in range(t_len):
        v = conv_b[...] + gbuf[t] * conv_w[0:1, :]
        for k in range(1, CONV_WIDTH):
            v = v + gbuf[t + k] * conv_w[k:k + 1, :]
        xc_rows.append(v)
    xc = jnp.concatenate(xc_rows, axis=0)
    for k in range(CONV_WIDTH - 1):
        conv_new_ref[k] = gbuf[len(gbuf) - (CONV_WIDTH - 1) + k]

    r, i = _gates(xc, gate_w, ba[...], bx[...])
    a, b = _lru_terms(xc, r, i, lam[...], None)
    h = h0_ref[...]
    h_rows = []
    for t in range(t_len):
        h = a[nb * t:nb * (t + 1), :] * h + b[nb * t:nb * (t + 1), :]
        h_rows.append(h)
    h_new_ref[...] = h
    rnn_out = jnp.concatenate(h_rows, axis=0) * jax.nn.gelu(gate)

    mix = _bdot(jnp.concatenate([pool_out, rnn_out], axis=-1), w_out[...])
    x1 = _layer_norm(ALPHA * xn + mix, ln1_g[...], ln1_b[...])
    cls, c_lo, c_hi = _route(x1, wr_hi, wr_lo, br[...])
    cnt_out_ref[...] = cnt_in_ref[...]
    rank = _rank_in_class(cls, cnt_out_ref)
    _write_token_rows(pay_ref, route_ref, x1, pe_ref[...], cls, rank, c_lo, c_hi)


def _moe_kernel(dest_ref, nv_ref, nvp_ref, ea_ref, eb_ref,
                payp_hbm, pays_hbm, wg_a, wu_a, wd_a, wg_b, wu_b, wd_b,
                ln2_g, ln2_b, w_pg, b_pg, w_ple, ln3_g, ln3_b,
                yp_hbm, ys_hbm, trash_hbm,
                src, gbuf, obuf, wga_bf, wua_bf, wda_bf, wgb_bf, wub_bf, wdb_bf, gsem, ssem,
                *, sample_batch, sample_seq):
    t = pl.program_id(0)
    wbf_a, wbf_b = (wga_bf, wua_bf, wda_bf), (wgb_bf, wub_bf, wdb_bf)

    prev = jnp.maximum(t - 1, 0)
    for e_ref, w_f32, w_bf in ((ea_ref, (wg_a, wu_a, wd_a), wbf_a), (eb_ref, (wg_b, wu_b, wd_b), wbf_b)):
        @pl.when((t == 0) | (e_ref[t] != e_ref[prev]))
        def _(w_f32=w_f32, w_bf=w_bf):
            for src_ref, dst_ref in zip(w_f32, w_bf):
                dst_ref[...] = src_ref[...].astype(BF16)
    wg_a, wu_a, wd_a = wbf_a
    wg_b, wu_b, wd_b = wbf_b

    n_tiles = pl.num_programs(0)
    tm = gbuf.shape[1] * SUBLANES
    slot = t % 2
    n_tok = dest_ref.shape[0]
    n_prompt = payp_hbm.shape[0]
    log2_batch = sample_batch.bit_length() - 1

    def rows8(tile):
        return pl.multiple_of((nv_ref[tile] + (SUBLANES - 1)) & -SUBLANES, SUBLANES)

    def buf_row(buf, slot_, j):
        return buf.at[slot_, lax.shift_right_logical(j, UNROLL_LOG2), pl.ds(j & (SUBLANES - 1), 1), :]

    def for_each_row(lo, hi, per_row):
        def row_body(j, c):
            per_row(j)
            return c
        lax.fori_loop(lo, hi, row_body, 0)

    def for_rows(hi, per_group, per_row):
        n_groups = lax.shift_right_logical(hi, UNROLL_LOG2)

        def group_body(g, c):
            per_group(g)
            return c
        lax.fori_loop(0, n_groups, group_body, 0)
        for_each_row(lax.shift_left(n_groups, UNROLL_LOG2), hi, per_row)

    def issue_gather(tile, slot_):
        base = tile * tm

        def from_prompt_group(g):
            for k in range(UNROLL):
                pltpu.make_async_copy(payp_hbm.at[pl.ds(src[base + g * UNROLL + k], 1), :],
                                      gbuf.at[slot_, g, pl.ds(k, 1), :], gsem.at[slot_]).start()

        def from_prompt(j):
            pltpu.make_async_copy(payp_hbm.at[pl.ds(src[base + j], 1), :], buf_row(gbuf, slot_, j),
                                  gsem.at[slot_]).start()

        def from_sample(j):
            pltpu.make_async_copy(pays_hbm.at[pl.ds(src[base + j] - n_prompt, 1), :], buf_row(gbuf, slot_, j),
                                  gsem.at[slot_]).start()

        def rounding(j):
            pltpu.make_async_copy(payp_hbm.at[pl.ds(0, 1), :], buf_row(gbuf, slot_, j), gsem.at[slot_]).start()

        nvp_, nv_ = nvp_ref[tile], nv_ref[tile]
        for_rows(nvp_, from_prompt_group, from_prompt)
        for_each_row(nvp_, nv_, from_sample)
        for_each_row(nv_, rows8(tile), rounding)

    def wait_gather(tile, slot_):
        n = rows8(tile)

        @pl.when(n > 0)
        def _():
            rows = payp_hbm.at[pl.ds(0, n), :]
            pltpu.make_async_copy(rows, rows, gsem.at[slot_]).wait()

    def wait_scatter(tile, slot_):
        n = rows8(tile)

        @pl.when(n > 0)
        def _():
            rows = yp_hbm.at[pl.ds(0, n), :]
            pltpu.make_async_copy(rows, rows, ssem.at[slot_]).wait()

    @pl.when(t == 0)
    def _():
        gbuf[...] = jnp.zeros_like(gbuf)
        init = pltpu.make_async_copy(gbuf.at[0, pl.ds(0, 2), :, pl.ds(0, D_MODEL)], trash_hbm, ssem.at[0])
        init.start()
        init.wait()

        def fill(g, c):
            for k in range(UNROLL):
                i = g * UNROLL + k
                src[dest_ref[i]] = i
            return c
        lax.fori_loop(0, n_tok // UNROLL, fill, 0)
        issue_gather(0, 0)

    @pl.when(t + 1 < n_tiles)
    def _():
        issue_gather(t + 1, 1 - slot)

    wait_gather(t, slot)

    @pl.when(t >= 2)
    def _():
        wait_scatter(t - 2, slot)

    nv = nv_ref[t]

    @pl.when(nv > 0)
    def _():
        xp = jnp.concatenate([gbuf[slot, g] for g in range(tm // SUBLANES)], axis=0)
        x1 = xp[:, :D_MODEL]
        meta = xp[:, D_MODEL:D_MODEL + LANES]
        pe = xp[:, D_MODEL + LANES:]
        xb = x1.astype(BF16)

        def expert(wg, wu, wd):
            hid = jax.nn.silu(jnp.dot(xb, wg[...], preferred_element_type=F32)) * jnp.dot(
                xb, wu[...], preferred_element_type=F32)
            return _bdot(hid, wd[...])

        moe = meta[:, 0:1] * expert(wg_a, wu_a, wd_a)
        moe = moe + meta[:, 1:2] * expert(wg_b, wu_b, wd_b)
        x2 = _layer_norm(ALPHA * x1 + moe, ln2_g[...], ln2_b[...])
        gate_p = jax.nn.sigmoid(_bdot(x2, w_pg[...]) + b_pg[...])
        ple = _bdot(pe, w_ple[...]) * gate_p
        y = _layer_norm(ALPHA * x2 + ple, ln3_g[...], ln3_b[...])
        for g in range(tm // SUBLANES):
            obuf[slot, g] = y[SUBLANES * g:SUBLANES * (g + 1), :]

    base = t * tm
    nvp = nvp_ref[t]

    def to_prompt_group(g):
        for k in range(UNROLL):
            pltpu.make_async_copy(obuf.at[slot, g, pl.ds(k, 1), :],
                                  yp_hbm.at[pl.ds(src[base + g * UNROLL + k], 1), :], ssem.at[slot]).start()

    def to_prompt(j):
        pltpu.make_async_copy(buf_row(obuf, slot, j), yp_hbm.at[pl.ds(src[base + j], 1), :], ssem.at[slot]).start()

    def to_sample(j):
        s = src[base + j] - n_prompt
        r = (s & (sample_batch - 1)) * sample_seq + lax.shift_right_logical(s, log2_batch)
        pltpu.make_async_copy(buf_row(obuf, slot, j), ys_hbm.at[pl.ds(r, 1), :], ssem.at[slot]).start()

    def to_trash(j):
        pltpu.make_async_copy(buf_row(obuf, slot, j), trash_hbm.at[slot, pl.ds(j - nv, 1), :], ssem.at[slot]).start()

    for_rows(nvp, to_prompt_group, to_prompt)
    for_each_row(nvp, nv, to_sample)
    for_each_row(nv, rows8(t), to_trash)

    @pl.when(t == n_tiles - 1)
    def _():
        @pl.when(t >= 1)
        def _():
            wait_scatter(t - 1, 1 - slot)
        wait_scatter(t, slot)


def _full(shape):
    nd = len(shape)
    return pl.BlockSpec(shape, lambda *_: (0,) * nd)


def _row(v):
    return v.reshape(1, -1).astype(F32)


def kernel(x_prompt, x_sample, state_pool, state_conv, state_h, p_prompt, p_sample, ln_in_g, ln_in_b, w_in,
           pool_w, pool_b, pool_scale, conv_w, conv_b, rg_wa, rg_ba, rg_wx, rg_bx, rg_lambda, w_out, ln1_g,
           ln1_b, w_rg, b_rg, w_re, b_re, w_eg, w_eu, w_ed, ln2_g, ln2_b, w_pg, b_pg, w_ple, ln3_g, ln3_b):
    bp, seq, _ = x_prompt.shape
    bs, dseq, _ = x_sample.shape
    n_p, n_s = bp * seq, bs * dseq
    n_tok = n_p + n_s
    assert seq % PROMPT_TILE == 0 and bp % PROMPT_STREAMS == 0 and n_tok % MOE_TILE == 0
    assert bs & (bs - 1) == 0, "sample batch must be a power of two"

    w_in_b = w_in[0].astype(BF16)
    w_out_b = w_out[0].astype(BF16)
    pool_w_b = pool_w[0].astype(BF16)
    zero = jnp.zeros((RNN_HEAD, RNN_HEAD), F32)

    def pair_block(w, j):
        top = jnp.concatenate([w[2 * j], zero], axis=1)
        bot = jnp.concatenate([zero, w[2 * j + 1]], axis=1)
        return jnp.concatenate([top, bot], axis=0)

    gate_w = jnp.stack([jnp.concatenate([pair_block(rg_wa[0], j), pair_block(rg_wx[0], j)], axis=1)
                        for j in range(N_RNN_HEADS // 2)]).astype(BF16)
    wr = jnp.concatenate([w_rg[0], w_re[0],
                          jnp.zeros((D_MODEL, LANES - N_EXPERT_GROUPS - N_EXPERTS), F32)], axis=1)
    wr_hi = wr.astype(BF16)
    wr_lo = (wr - wr_hi.astype(F32)).astype(BF16)
    br = jnp.concatenate([b_rg[0], b_re[0],
                          jnp.zeros((ROUTER_ROWS - N_EXPERT_GROUPS - N_EXPERTS,), F32)]).reshape(ROUTER_ROWS, 1)

    mixer_params = (_row(ln_in_g), _row(ln_in_b), w_in_b, pool_w_b, _row(pool_b[0]), _row(pool_scale[0]),
                    conv_w[0], _row(conv_b[0]), gate_w, _row(rg_ba[0]), _row(rg_bx[0]), _row(rg_lambda[0]),
                    w_out_b, _row(ln1_g[0]), _row(ln1_b[0]), wr_hi, wr_lo, br)
    mixer_specs = [_full(p.shape) for p in mixer_params]

    n_t = seq // PROMPT_TILE
    nb = PROMPT_STREAMS
    pay_p, route_p, cnt_p, pool_p, conv_p, h_p = pl.pallas_call(
        _prompt_kernel,
        grid=(bp // nb, n_t),
        in_specs=[pl.BlockSpec((nb, PROMPT_TILE, D_MODEL), lambda b, t: (b, t, 0)),
                  pl.BlockSpec((None, nb, PROMPT_TILE, D_PLE), lambda b, t: (0, b, t, 0))] + mixer_specs,
        out_specs=[
            pl.BlockSpec((nb, PROMPT_TILE, D_PAY), lambda b, t: (b, t, 0)),
            pl.BlockSpec((nb, SUBLANES, PROMPT_TILE), lambda b, t: (b, 0, t)),
            _full((CLASS_ROWS, 1)),
            pl.BlockSpec((nb, POOL_HALO, D_POOL), lambda b, t: (b, 0, 0)),
            pl.BlockSpec((nb, CONV_HALO, D_RNN), lambda b, t: (b, 0, 0)),
            pl.BlockSpec((nb, SUBLANES, D_RNN), lambda b, t: (b, 0, 0)),
        ],
        out_shape=[
            jax.ShapeDtypeStruct((bp, seq, D_PAY), F32),
            jax.ShapeDtypeStruct((bp, SUBLANES, seq), F32),
            jax.ShapeDtypeStruct((CLASS_ROWS, 1), F32),
            jax.ShapeDtypeStruct((bp, POOL_HALO, D_POOL), F32),
            jax.ShapeDtypeStruct((bp, CONV_HALO, D_RNN), F32),
            jax.ShapeDtypeStruct((bp, SUBLANES, D_RNN), F32),
        ],
        scratch_shapes=[pltpu.VMEM((nb, POOL_HALO, D_POOL), F32), pltpu.VMEM((nb, CONV_HALO, D_RNN), F32),
                        pltpu.VMEM((nb, 1, D_RNN), F32), pltpu.VMEM((CLASS_ROWS, 1), F32)],
        compiler_params=pltpu.CompilerParams(dimension_semantics=("arbitrary", "arbitrary"),
                                             vmem_limit_bytes=VMEM_LIMIT),
        name="prompt_mixers",
    )(x_prompt, p_prompt, *mixer_params)
    pay_p = pay_p.reshape(n_p, D_PAY)

    xs_tm = x_sample.transpose(1, 0, 2).reshape(n_s, D_MODEL)
    pe_tm = p_sample[0].transpose(1, 0, 2).reshape(n_s, D_PLE)
    pool_buf_tm = state_pool[0].transpose(1, 0, 2)
    conv_buf_tm = state_conv[0].transpose(1, 0, 2)
    sample_in = (xs_tm, pe_tm, pool_buf_tm, conv_buf_tm, state_h[0], cnt_p)
    pay_s, route_s, cnt_all, pool_s, conv_s, h_s = pl.pallas_call(
        _sample_kernel,
        grid=(1,),
        in_specs=[_full(p.shape) for p in sample_in] + mixer_specs,
        out_specs=[_full((n_s, D_PAY)), _full((SUBLANES, n_s)),
                   _full((CLASS_ROWS, 1)), _full((POOL_BUF, bs, D_POOL)),
                   _full((CONV_WIDTH - 1, bs, D_RNN)), _full((bs, D_RNN))],
        out_shape=[
            jax.ShapeDtypeStruct((n_s, D_PAY), F32),
            jax.ShapeDtypeStruct((SUBLANES, n_s), F32),
            jax.ShapeDtypeStruct((CLASS_ROWS, 1), F32),
            jax.ShapeDtypeStruct((POOL_BUF, bs, D_POOL), F32),
            jax.ShapeDtypeStruct((CONV_WIDTH - 1, bs, D_RNN), F32),
            jax.ShapeDtypeStruct((bs, D_RNN), F32),
        ],
        compiler_params=pltpu.CompilerParams(dimension_semantics=("arbitrary",), vmem_limit_bytes=VMEM_LIMIT),
        name="sample_mixers",
    )(*sample_in, *mixer_params)

    tm = MOE_TILE
    n_tiles = n_tok // tm + N_CLASSES
    counts = cnt_all[:N_CLASSES, 0].astype(I32)
    counts_p = cnt_p[:N_CLASSES, 0].astype(I32)
    tiles_c = (counts + tm - 1) // tm
    tile_end = jnp.cumsum(tiles_c)
    tile_start = tile_end - tiles_c
    row_start = tile_start * tm
    cls_all = jnp.concatenate([route_p[:, 0, :].reshape(n_p), route_s[0]]).astype(I32)
    rank_all = jnp.concatenate([route_p[:, 1, :].reshape(n_p), route_s[1]]).astype(I32)
    dest = row_start[cls_all] + rank_all
    tile_id = jnp.arange(n_tiles, dtype=I32)
    tile_cls = jnp.minimum(jnp.sum((tile_id[:, None] >= tile_end[None, :]).astype(I32), axis=1), N_CLASSES - 1)
    used = tile_id < tile_end[-1]
    first_row = (tile_id - tile_start[tile_cls]) * tm
    nv = jnp.where(used, jnp.clip(counts[tile_cls] - first_row, 0, tm), 0).astype(I32)
    nvp = jnp.minimum(jnp.clip(counts_p[tile_cls] - first_row, 0, tm), nv).astype(I32)
    pair_lo = jnp.array([p[0] for p in PAIRS], I32)
    pair_hi = jnp.array([p[1] for p in PAIRS], I32)
    tile_group = tile_cls // len(PAIRS)
    ea = (tile_group * EXPERTS_PER_GROUP + pair_lo[tile_cls % len(PAIRS)]).astype(I32)
    eb = (tile_group * EXPERTS_PER_GROUP + pair_hi[tile_cls % len(PAIRS)]).astype(I32)

    moe_params = (_row(ln2_g[0]), _row(ln2_b[0]), w_pg[0].astype(BF16), _row(b_pg[0]), w_ple[0].astype(BF16),
                  _row(ln3_g[0]), _row(ln3_b[0]))
    up_spec_a = pl.BlockSpec((None, D_MODEL, D_EXPERT), lambda t, d, n, p, a, b: (a[t], 0, 0))
    up_spec_b = pl.BlockSpec((None, D_MODEL, D_EXPERT), lambda t, d, n, p, a, b: (b[t], 0, 0))
    dn_spec_a = pl.BlockSpec((None, D_EXPERT, D_MODEL), lambda t, d, n, p, a, b: (a[t], 0, 0))
    dn_spec_b = pl.BlockSpec((None, D_EXPERT, D_MODEL), lambda t, d, n, p, a, b: (b[t], 0, 0))
    y_p, y_s, _ = pl.pallas_call(
        functools.partial(_moe_kernel, sample_batch=bs, sample_seq=dseq),
        grid_spec=pltpu.PrefetchScalarGridSpec(
            num_scalar_prefetch=5,
            grid=(n_tiles,),
            in_specs=[pl.BlockSpec(memory_space=pl.ANY), pl.BlockSpec(memory_space=pl.ANY),
                      up_spec_a, up_spec_a, dn_spec_a,
                      up_spec_b, up_spec_b, dn_spec_b] + [_full(p.shape) for p in moe_params],
            out_specs=[pl.BlockSpec(memory_space=pl.ANY)] * 3,
            scratch_shapes=[pltpu.SMEM((n_tiles * tm,), I32),
                            pltpu.VMEM((2, tm // SUBLANES, SUBLANES, D_PAY), F32),
                            pltpu.VMEM((2, tm // SUBLANES, SUBLANES, D_MODEL), F32)]
            + [pltpu.VMEM((D_MODEL, D_EXPERT), BF16), pltpu.VMEM((D_MODEL, D_EXPERT), BF16),
               pltpu.VMEM((D_EXPERT, D_MODEL), BF16)] * 2
            + [pltpu.SemaphoreType.DMA((2,)), pltpu.SemaphoreType.DMA((2,))],
        ),
        out_shape=[jax.ShapeDtypeStruct((n_p, D_MODEL), F32), jax.ShapeDtypeStruct((n_s, D_MODEL), F32),
                   jax.ShapeDtypeStruct((2, SUBLANES, D_MODEL), F32)],
        compiler_params=pltpu.CompilerParams(dimension_semantics=("arbitrary",), vmem_limit_bytes=VMEM_LIMIT),
        name="moe_embed",
    )(dest, nv, nvp, ea, eb, pay_p, pay_s, w_eg[0], w_eu[0], w_ed[0], w_eg[0], w_eu[0], w_ed[0], *moe_params)

    return (y_p.reshape(bp, seq, D_MODEL), y_s.reshape(bs, dseq, D_MODEL),
            pool_p[None, :, POOL_HALO - POOL_BUF:, :], conv_p[None, :, CONV_HALO - (CONV_WIDTH - 1):, :],
            h_p[None, :, 0, :],
            pool_s.transpose(1, 0, 2)[None], conv_s.transpose(1, 0, 2)[None], h_s[None])
```

```python
import functools

import jax
import jax.numpy as jnp
from jax import lax
from jax.experimental import pallas as pl
from jax.experimental.pallas import tpu as pltpu

F32 = jnp.float32
BF16 = jnp.bfloat16
I32 = jnp.int32

D_MODEL = 1024
D_POOL = 512
D_RNN = 512
POOL_WINDOWS = (2, 4, 8, 16)
POOL_GROUP = 128
POOL_BUF = 15
N_RNN_HEADS = 8
RNN_HEAD = 64
CONV_WIDTH = 4
LRU_C = 8.0
N_EXPERT_GROUPS = 4
EXPERTS_PER_GROUP = 4
N_EXPERTS = 16
D_EXPERT = 512
D_PLE = 256
LN_EPS = 1e-5
DEPTH = 1
ALPHA = (2 * DEPTH) ** 0.25
PAST_LEN = 16384

LANES = 128
SUBLANES = 8
ROUTER_ROWS = 32
PAIRS = ((0, 1), (0, 2), (0, 3), (1, 2), (1, 3), (2, 3))
N_CLASSES = N_EXPERT_GROUPS * len(PAIRS)
CLASS_ROWS = 32
POOL_HALO = 16
CONV_HALO = 8
PROMPT_TILE = 256
PROMPT_STREAMS = 2
MOE_TILE = 384
UNROLL = SUBLANES
UNROLL_LOG2 = 3
D_PAY = D_MODEL + LANES + D_PLE
VMEM_LIMIT = 56 * 1024 * 1024


def _layer_norm(x, g, b):
    mu = jnp.mean(x, axis=-1, keepdims=True)
    xc = x - mu
    var = jnp.mean(xc * xc, axis=-1, keepdims=True)
    return xc * lax.rsqrt(var + LN_EPS) * g + b


def _bdot(a, w):
    return jnp.dot(a.astype(BF16), w, preferred_element_type=F32)


def _gates(xc, gate_w_ref, ba, bx):
    r_parts, i_parts = [], []
    for j in range(D_RNN // LANES):
        o = _bdot(xc[:, LANES * j:LANES * (j + 1)], gate_w_ref[j])
        r_parts.append(o[:, :LANES])
        i_parts.append(o[:, LANES:])
    r = jax.nn.sigmoid(jnp.concatenate(r_parts, axis=-1) + ba)
    i = jax.nn.sigmoid(jnp.concatenate(i_parts, axis=-1) + bx)
    return r, i


def _lru_terms(xc, r, i, lam, first_pos_mask):
    log_a = -LRU_C * r * jax.nn.softplus(-lam)
    a = jnp.exp(log_a)
    mult = jnp.sqrt(jnp.tanh(-log_a) * (a * a + 1.0))
    if first_pos_mask is not None:
        mult = jnp.where(first_pos_mask, 1.0, mult)
    return a, mult * i * xc


def _pool_project(d, pool_w_ref, pool_b, pool_scale):
    outs = []
    for g in range(len(POOL_WINDOWS)):
        outs.append(_bdot(d[:, POOL_GROUP * g:POOL_GROUP * (g + 1)], pool_w_ref[g]))
    return (jnp.concatenate(outs, axis=-1) + pool_b) * pool_scale


def _route(x1, wr_hi_ref, wr_lo_ref, br):
    hi = x1.astype(BF16)
    lo = (x1 - hi.astype(F32)).astype(BF16)
    logits_tm = (jnp.dot(hi, wr_hi_ref[...], preferred_element_type=F32)
                 + jnp.dot(lo, wr_hi_ref[...], preferred_element_type=F32)
                 + jnp.dot(hi, wr_lo_ref[...], preferred_element_type=F32))
    logits = logits_tm.T[:ROUTER_ROWS, :] + br
    lg = [logits[j:j + 1, :] for j in range(N_EXPERT_GROUPS)]
    m = functools.reduce(jnp.maximum, lg)
    ex = [jnp.exp(v - m) for v in lg]
    den = functools.reduce(lambda p, q: p + q, ex)
    gp = [v / den for v in ex]
    g_w, g_idx = gp[0], jnp.zeros_like(gp[0], dtype=I32)
    for j in range(1, N_EXPERT_GROUPS):
        upd = gp[j] > g_w
        g_idx = jnp.where(upd, j, g_idx)
        g_w = jnp.where(upd, gp[j], g_w)
    e_in = []
    for k in range(EXPERTS_PER_GROUP):
        v = jnp.zeros_like(g_w)
        for g in range(N_EXPERT_GROUPS):
            row = N_EXPERT_GROUPS + g * EXPERTS_PER_GROUP + k
            v = v + jnp.where(g_idx == g, logits[row:row + 1, :], 0.0)
        e_in.append(v)
    m = functools.reduce(jnp.maximum, e_in)
    ex = [jnp.exp(v - m) for v in e_in]
    den = functools.reduce(lambda p, q: p + q, ex)
    ep = [v / den for v in ex]
    w0, i0 = ep[0], jnp.zeros_like(g_idx)
    for k in range(1, EXPERTS_PER_GROUP):
        upd = ep[k] > w0
        i0 = jnp.where(upd, k, i0)
        w0 = jnp.where(upd, ep[k], w0)
    w1, i1 = jnp.full_like(w0, -1.0), jnp.zeros_like(g_idx)
    for k in range(EXPERTS_PER_GROUP):
        upd = (ep[k] > w1) & (i0 != k)
        i1 = jnp.where(upd, k, i1)
        w1 = jnp.where(upd, ep[k], w1)
    wsum = w0 + w1
    c0 = g_w * (w0 / wsum)
    c1 = g_w * (w1 / wsum)
    first_is_lo = i0 < i1
    e_lo = jnp.minimum(i0, i1)
    e_hi = jnp.maximum(i0, i1)
    pair = jnp.zeros_like(g_idx)
    for p, (a, b) in enumerate(PAIRS):
        pair = jnp.where((e_lo == a) & (e_hi == b), p, pair)
    cls = g_idx * len(PAIRS) + pair
    return cls, jnp.where(first_is_lo, c0, c1), jnp.where(first_is_lo, c1, c0)


def _rank_in_class(cls, cnt_ref):
    t = cls.shape[1]
    onehot = (lax.broadcasted_iota(I32, (CLASS_ROWS, t), 0) == cls).astype(F32)
    before = (lax.broadcasted_iota(I32, (t, t), 0) < lax.broadcasted_iota(I32, (t, t), 1)).astype(BF16)
    prefix = jnp.dot(onehot.astype(BF16), before, preferred_element_type=F32)
    rank = jnp.sum(onehot * (prefix + cnt_ref[...]), axis=0, keepdims=True)
    cnt_ref[...] = cnt_ref[...] + jnp.sum(onehot, axis=1, keepdims=True)
    return rank


def _write_token_rows(pay_ref, route_ref, x1, pe, cls, rank, c_lo, c_hi):
    t = x1.shape[0]
    meta_t = jnp.concatenate([c_lo, c_hi, jnp.zeros((LANES - 2, t), F32)], axis=0)
    pay_ref[:, :D_MODEL] = x1
    pay_ref[:, D_MODEL:D_MODEL + LANES] = meta_t.T
    pay_ref[:, D_MODEL + LANES:] = pe
    route_ref[...] = jnp.concatenate([cls.astype(F32), rank, jnp.zeros((SUBLANES - 2, t), F32)], axis=0)


def _scan_rows(a, b, h0):
    t, c = a.shape
    sub = lax.broadcasted_iota(I32, (SUBLANES, c), 0)
    keep = {s: sub >= s for s in (1, 2, 4)}
    h, out = h0, []
    for g in range(t // SUBLANES):
        ag = a[SUBLANES * g:SUBLANES * (g + 1), :]
        bg = b[SUBLANES * g:SUBLANES * (g + 1), :]
        for s in (1, 2, 4):
            a_prev = jnp.where(keep[s], pltpu.roll(ag, s, 0), 1.0)
            b_prev = jnp.where(keep[s], pltpu.roll(bg, s, 0), 0.0)
            bg = ag * b_prev + bg
            ag = ag * a_prev
        hg = ag * h + bg
        h = hg[SUBLANES - 1:SUBLANES, :]
        out.append(hg)
    return jnp.concatenate(out, axis=0)


def _shift_rows(x, s):
    return pltpu.roll(x, s, 0)


def _prompt_kernel(x_ref, pe_ref, ln_in_g, ln_in_b, w_in, pool_w, pool_b, pool_scale, conv_w, conv_b,
                   gate_w, ba, bx, lam, w_out, ln1_g, ln1_b, wr_hi, wr_lo, br,
                   pay_ref, route_ref, cnt_out_ref, pool_new_ref, conv_new_ref, h_new_ref,
                   pool_halo, conv_halo, h_carry, cnt):
    t_idx = pl.program_id(1)

    @pl.when((pl.program_id(0) == 0) & (t_idx == 0))
    def _():
        cnt[...] = jnp.zeros_like(cnt)

    @pl.when(t_idx == 0)
    def _():
        pool_halo[...] = jnp.zeros_like(pool_halo)
        conv_halo[...] = jnp.zeros_like(conv_halo)
        h_carry[...] = jnp.zeros_like(h_carry)

    for s in range(x_ref.shape[0]):
        _prompt_tile(t_idx, x_ref.at[s], pe_ref.at[s], ln_in_g, ln_in_b, w_in, pool_w, pool_b, pool_scale,
                     conv_w, conv_b, gate_w, ba, bx, lam, w_out, ln1_g, ln1_b, wr_hi, wr_lo, br,
                     pay_ref.at[s], route_ref.at[s], pool_new_ref.at[s], conv_new_ref.at[s], h_new_ref.at[s],
                     pool_halo.at[s], conv_halo.at[s], h_carry.at[s], cnt)
    cnt_out_ref[...] = cnt[...]


def _prompt_tile(t_idx, x_ref, pe_ref, ln_in_g, ln_in_b, w_in, pool_w, pool_b, pool_scale, conv_w, conv_b,
                 gate_w, ba, bx, lam, w_out, ln1_g, ln1_b, wr_hi, wr_lo, br,
                 pay_ref, route_ref, pool_new_ref, conv_new_ref, h_new_ref,
                 pool_halo, conv_halo, h_carry, cnt):
    tt = x_ref.shape[0]
    xn = _layer_norm(x_ref[...], ln_in_g[...], ln_in_b[...])
    proj = _bdot(xn, w_in[...])
    u = proj[:, :D_POOL]
    xr = proj[:, D_POOL:D_POOL + D_RNN]
    gate = proj[:, D_POOL + D_RNN:]

    row = lax.broadcasted_iota(I32, (tt, 1), 0)
    pos = t_idx * tt + row

    e = jnp.concatenate([pool_halo[...], u], axis=0)
    s = e + _shift_rows(e, 1)
    sums = [s[:, :POOL_GROUP]]
    s = s[:, POOL_GROUP:]
    for step in (2, 4, 8):
        s = s + _shift_rows(s, step)
        sums.append(s[:, :POOL_GROUP])
        s = s[:, POOL_GROUP:]
    d_parts = []
    for g, win in enumerate(POOL_WINDOWS):
        cnt_w = jnp.minimum(pos + 1, win).astype(F32)
        d_parts.append(sums[g][POOL_HALO:, :] / cnt_w - u[:, POOL_GROUP * g:POOL_GROUP * (g + 1)])
    pool_out = _pool_project(jnp.concatenate(d_parts, axis=-1), pool_w, pool_b[...], pool_scale[...])
    pool_halo[...] = u[tt - POOL_HALO:, :]

    ec = jnp.concatenate([conv_halo[...], xr], axis=0)
    xc = conv_b[...] + _shift_rows(ec, 3) * conv_w[0:1, :]
    xc = xc + _shift_rows(ec, 2) * conv_w[1:2, :]
    xc = xc + _shift_rows(ec, 1) * conv_w[2:3, :]
    xc = (xc + ec * conv_w[3:4, :])[CONV_HALO:, :]
    conv_halo[...] = xr[tt - CONV_HALO:, :]

    r, i = _gates(xc, gate_w, ba[...], bx[...])
    a, b = _lru_terms(xc, r, i, lam[...], pos == 0)
    h = _scan_rows(a, b, h_carry[...])
    h_carry[...] = h[tt - 1:tt, :]
    rnn_out = h * jax.nn.gelu(gate)

    mix = _bdot(jnp.concatenate([pool_out, rnn_out], axis=-1), w_out[...])
    x1 = _layer_norm(ALPHA * xn + mix, ln1_g[...], ln1_b[...])
    cls, c_lo, c_hi = _route(x1, wr_hi, wr_lo, br[...])
    rank = _rank_in_class(cls, cnt)
    _write_token_rows(pay_ref, route_ref, x1, pe_ref[...], cls, rank, c_lo, c_hi)
    pool_new_ref[...] = u[tt - POOL_HALO:, :]
    conv_new_ref[...] = xr[tt - CONV_HALO:, :]
    h_new_ref[...] = jnp.broadcast_to(h[tt - 1:tt, :], h_new_ref.shape)


def _sample_kernel(x_ref, pe_ref, pool_buf_ref, conv_buf_ref, h0_ref, cnt_in_ref,
                   ln_in_g, ln_in_b, w_in, pool_w, pool_b, pool_scale, conv_w, conv_b,
                   gate_w, ba, bx, lam, w_out, ln1_g, ln1_b, wr_hi, wr_lo, br,
                   pay_ref, route_ref, cnt_out_ref, pool_new_ref, conv_new_ref, h_new_ref):
    nb = h0_ref.shape[0]
    t_len = x_ref.shape[0] // nb
    xn = _layer_norm(x_ref[...], ln_in_g[...], ln_in_b[...])
    proj = _bdot(xn, w_in[...])
    u = proj[:, :D_POOL]
    xr = proj[:, D_POOL:D_POOL + D_RNN]
    gate = proj[:, D_POOL + D_RNN:]

    f = [pool_buf_ref[k] for k in range(POOL_BUF)] + [u[nb * t:nb * (t + 1), :] for t in range(t_len)]
    n_rows = len(f)
    s = [f[0]] + [f[k] + f[k - 1] for k in range(1, n_rows)]
    sums = [[v[:, :POOL_GROUP] for v in s]]
    s = [v[:, POOL_GROUP:] for v in s]
    for step in (2, 4, 8):
        s = [s[k] + s[k - step] if k >= step else s[k] for k in range(n_rows)]
        sums.append([v[:, :POOL_GROUP] for v in s])
        s = [v[:, POOL_GROUP:] for v in s]
    d_rows = []
    for t in range(t_len):
        parts = []
        for g, win in enumerate(POOL_WINDOWS):
            cnt_w = float(min(PAST_LEN + t + 1, win))
            parts.append(sums[g][POOL_BUF + t] / cnt_w - f[POOL_BUF + t][:, POOL_GROUP * g:POOL_GROUP * (g + 1)])
        d_rows.append(jnp.concatenate(parts, axis=-1))
    pool_out = _pool_project(jnp.concatenate(d_rows, axis=0), pool_w, pool_b[...], pool_scale[...])
    for k in range(POOL_BUF):
        pool_new_ref[k] = f[n_rows - POOL_BUF + k]

    gbuf = [conv_buf_ref[k] for k in range(CONV_WIDTH - 1)] + [xr[nb * t:nb * (t + 1), :] for t in range(t_len)]
    xc_rows = []
    for t in range(t_len):
        v = conv_b[...] + gbuf[t] * conv_w[0:1, :]
        for k in range(1, CONV_WIDTH):
            v = v + gbuf[t + k] * conv_w[k:k + 1, :]
        xc_rows.append(v)
    xc = jnp.concatenate(xc_rows, axis=0)
    for k in range(CONV_WIDTH - 1):
        conv_new_ref[k] = gbuf[len(gbuf) - (CONV_WIDTH - 1) + k]

    r, i = _gates(xc, gate_w, ba[...], bx[...])
    a, b = _lru_terms(xc, r, i, lam[...], None)
    h = h0_ref[...]
    h_rows = []
    for t in range(t_len):
        h = a[nb * t:nb * (t + 1), :] * h + b[nb * t:nb * (t + 1), :]
        h_rows.append(h)
    h_new_ref[...] = h
    rnn_out = jnp.concatenate(h_rows, axis=0) * jax.nn.gelu(gate)

    mix = _bdot(jnp.concatenate([pool_out, rnn_out], axis=-1), w_out[...])
    x1 = _layer_norm(ALPHA * xn + mix, ln1_g[...], ln1_b[...])
    cls, c_lo, c_hi = _route(x1, wr_hi, wr_lo, br[...])
    cnt_out_ref[...] = cnt_in_ref[...]
    rank = _rank_in_class(cls, cnt_out_ref)
    _write_token_rows(pay_ref, route_ref, x1, pe_ref[...], cls, rank, c_lo, c_hi)


def _moe_kernel(dest_ref, nv_ref, nvp_ref, ea_ref, eb_ref,
                payp_hbm, pays_hbm, wg_a, wu_a, wd_a, wg_b, wu_b, wd_b,
                ln2_g, ln2_b, w_pg, b_pg, w_ple, ln3_g, ln3_b,
                yp_hbm, ys_hbm,
                src, gbuf0, gbuf1, obuf0, obuf1, wga_bf, wua_bf, wda_bf, wgb_bf, wub_bf, wdb_bf, gsem, ssem,
                *, sample_batch, sample_seq):
    t = pl.program_id(0)
    wbf_a, wbf_b = (wga_bf, wua_bf, wda_bf), (wgb_bf, wub_bf, wdb_bf)

    prev = jnp.maximum(t - 1, 0)
    for e_ref, w_f32, w_bf in ((ea_ref, (wg_a, wu_a, wd_a), wbf_a), (eb_ref, (wg_b, wu_b, wd_b), wbf_b)):
        @pl.when((t == 0) | (e_ref[t] != e_ref[prev]))
        def _(w_f32=w_f32, w_bf=w_bf):
            for src_ref, dst_ref in zip(w_f32, w_bf):
                dst_ref[...] = src_ref[...].astype(BF16)
    wg_a, wu_a, wd_a = wbf_a
    wg_b, wu_b, wd_b = wbf_b

    n_tiles = pl.num_programs(0)
    n_groups = gbuf0.shape[0]
    tm = n_groups * SUBLANES
    n_tok = dest_ref.shape[0]
    n_prompt = payp_hbm.shape[0]
    log2_batch = sample_batch.bit_length() - 1
    gbufs, obufs = (gbuf0, gbuf1), (obuf0, obuf1)

    def buf_row(buf, j):
        if isinstance(j, int):
            return buf.at[j // SUBLANES, pl.ds(j % SUBLANES, 1), :]
        return buf.at[lax.shift_right_logical(j, UNROLL_LOG2), pl.ds(j & (SUBLANES - 1), 1), :]

    def for_each_row(lo, hi, per_row):
        def row_body(j, c):
            per_row(j)
            return c
        lax.fori_loop(lo, hi, row_body, 0)

    def gather_prompt_row(tile, par, j):
        pltpu.make_async_copy(payp_hbm.at[pl.ds(src[tile * tm + j], 1), :], buf_row(gbufs[par], j),
                              gsem.at[par]).start()

    def gather_sample_rows(tile, par):
        def from_sample(j):
            pltpu.make_async_copy(pays_hbm.at[pl.ds(src[tile * tm + j] - n_prompt, 1), :], buf_row(gbufs[par], j),
                                  gsem.at[par]).start()
        for_each_row(nvp_ref[tile], nv_ref[tile], from_sample)

    def scatter_prompt_row(tile, par, j):
        pltpu.make_async_copy(buf_row(obufs[par], j), yp_hbm.at[pl.ds(src[tile * tm + j], 1), :],
                              ssem.at[par]).start()

    def scatter_sample_rows(tile, par):
        def to_sample(j):
            s = src[tile * tm + j] - n_prompt
            r = (s & (sample_batch - 1)) * sample_seq + lax.shift_right_logical(s, log2_batch)
            pltpu.make_async_copy(buf_row(obufs[par], j), ys_hbm.at[pl.ds(r, 1), :], ssem.at[par]).start()
        for_each_row(nvp_ref[tile], nv_ref[tile], to_sample)

    def wait_rows(rows_hbm, n, sem):
        n8 = pl.multiple_of(n & -SUBLANES, SUBLANES)

        @pl.when(n8 > 0)
        def _():
            rows = rows_hbm.at[pl.ds(0, n8), :]
            pltpu.make_async_copy(rows, rows, sem).wait()

        def one(j):
            row = rows_hbm.at[pl.ds(0, 1), :]
            pltpu.make_async_copy(row, row, sem).wait()
        for_each_row(n8, n, one)

    def compute(par, overlapped_issue):
        gbuf, obuf = gbufs[par], obufs[par]
        xp = jnp.concatenate([gbuf[g] for g in range(n_groups)], axis=0)
        x1 = xp[:, :D_MODEL]
        meta = xp[:, D_MODEL:D_MODEL + LANES]
        pe = xp[:, D_MODEL + LANES:]
        xb = x1.astype(BF16)
        overlapped_issue()

        def expert(wg, wu, wd):
            hid = jax.nn.silu(jnp.dot(xb, wg[...], preferred_element_type=F32)) * jnp.dot(
                xb, wu[...], preferred_element_type=F32)
            return _bdot(hid, wd[...])

        moe = meta[:, 0:1] * expert(wg_a, wu_a, wd_a)
        moe = moe + meta[:, 1:2] * expert(wg_b, wu_b, wd_b)
        x2 = _layer_norm(ALPHA * x1 + moe, ln2_g[...], ln2_b[...])
        gate_p = jax.nn.sigmoid(_bdot(x2, w_pg[...]) + b_pg[...])
        ple = _bdot(pe, w_ple[...]) * gate_p
        y = _layer_norm(ALPHA * x2 + ple, ln3_g[...], ln3_b[...])
        for g in range(n_groups):
            obuf[g] = y[SUBLANES * g:SUBLANES * (g + 1), :]

    @pl.when(t == 0)
    def _():
        gbuf0[...] = jnp.zeros_like(gbuf0)
        gbuf1[...] = jnp.zeros_like(gbuf1)

        def fill(g, c):
            for k in range(UNROLL):
                i = g * UNROLL + k
                src[dest_ref[i]] = i
            return c
        lax.fori_loop(0, n_tok // UNROLL, fill, 0)
        for_each_row(0, nvp_ref[0], functools.partial(gather_prompt_row, 0, 0))
        gather_sample_rows(0, 0)

    nxt = jnp.minimum(t + 1, n_tiles - 1)
    prv = jnp.maximum(t - 1, 0)
    n_gather_next = jnp.where(t + 1 < n_tiles, nvp_ref[nxt], 0)
    n_scatter_prev = jnp.where(t >= 1, nvp_ref[prv], 0)

    def step(par):
        oth = 1 - par
        wait_rows(payp_hbm, nv_ref[t], gsem.at[par])

        @pl.when(t >= 2)
        def _():
            wait_rows(yp_hbm, nv_ref[jnp.maximum(t - 2, 0)], ssem.at[par])

        def overlapped_issue():
            for j in range(tm):
                @pl.when(j < n_gather_next)
                def _(j=j):
                    gather_prompt_row(nxt, oth, j)

                @pl.when(j < n_scatter_prev)
                def _(j=j):
                    scatter_prompt_row(prv, oth, j)

        @pl.when(nv_ref[t] > 0)
        def _():
            compute(par, overlapped_issue)

        @pl.when(nv_ref[t] <= 0)
        def _():
            for_each_row(0, n_gather_next, functools.partial(gather_prompt_row, nxt, oth))
            for_each_row(0, n_scatter_prev, functools.partial(scatter_prompt_row, prv, oth))

        @pl.when(t + 1 < n_tiles)
        def _():
            gather_sample_rows(nxt, oth)

        @pl.when(t >= 1)
        def _():
            scatter_sample_rows(prv, oth)

        @pl.when(t == n_tiles - 1)
        def _():
            wait_rows(yp_hbm, nv_ref[prv], ssem.at[oth])

    @pl.when(t % 2 == 0)
    def _():
        step(0)

    @pl.when(t % 2 == 1)
    def _():
        step(1)


def _full(shape):
    nd = len(shape)
    return pl.BlockSpec(shape, lambda *_: (0,) * nd)


def _row(v):
    return v.reshape(1, -1).astype(F32)


def kernel(x_prompt, x_sample, state_pool, state_conv, state_h, p_prompt, p_sample, ln_in_g, ln_in_b, w_in,
           pool_w, pool_b, pool_scale, conv_w, conv_b, rg_wa, rg_ba, rg_wx, rg_bx, rg_lambda, w_out, ln1_g,
           ln1_b, w_rg, b_rg, w_re, b_re, w_eg, w_eu, w_ed, ln2_g, ln2_b, w_pg, b_pg, w_ple, ln3_g, ln3_b):
    bp, seq, _ = x_prompt.shape
    bs, dseq, _ = x_sample.shape
    n_p, n_s = bp * seq, bs * dseq
    n_tok = n_p + n_s
    assert seq % PROMPT_TILE == 0 and bp % PROMPT_STREAMS == 0 and n_tok % MOE_TILE == 0
    assert bs & (bs - 1) == 0, "sample batch must be a power of two"

    w_in_b = w_in[0].astype(BF16)
    w_out_b = w_out[0].astype(BF16)
    pool_w_b = pool_w[0].astype(BF16)
    zero = jnp.zeros((RNN_HEAD, RNN_HEAD), F32)

    def pair_block(w, j):
        top = jnp.concatenate([w[2 * j], zero], axis=1)
        bot = jnp.concatenate([zero, w[2 * j + 1]], axis=1)
        return jnp.concatenate([top, bot], axis=0)

    gate_w = jnp.stack([jnp.concatenate([pair_block(rg_wa[0], j), pair_block(rg_wx[0], j)], axis=1)
                        for j in range(N_RNN_HEADS // 2)]).astype(BF16)
    wr = jnp.concatenate([w_rg[0], w_re[0],
                          jnp.zeros((D_MODEL, LANES - N_EXPERT_GROUPS - N_EXPERTS), F32)], axis=1)
    wr_hi = wr.astype(BF16)
    wr_lo = (wr - wr_hi.astype(F32)).astype(BF16)
    br = jnp.concatenate([b_rg[0], b_re[0],
                          jnp.zeros((ROUTER_ROWS - N_EXPERT_GROUPS - N_EXPERTS,), F32)]).reshape(ROUTER_ROWS, 1)

    mixer_params = (_row(ln_in_g), _row(ln_in_b), w_in_b, pool_w_b, _row(pool_b[0]), _row(pool_scale[0]),
                    conv_w[0], _row(conv_b[0]), gate_w, _row(rg_ba[0]), _row(rg_bx[0]), _row(rg_lambda[0]),
                    w_out_b, _row(ln1_g[0]), _row(ln1_b[0]), wr_hi, wr_lo, br)
    mixer_specs = [_full(p.shape) for p in mixer_params]

    n_t = seq // PROMPT_TILE
    nb = PROMPT_STREAMS
    pay_p, route_p, cnt_p, pool_p, conv_p, h_p = pl.pallas_call(
        _prompt_kernel,
        grid=(bp // nb, n_t),
        in_specs=[pl.BlockSpec((nb, PROMPT_TILE, D_MODEL), lambda b, t: (b, t, 0)),
                  pl.BlockSpec((None, nb, PROMPT_TILE, D_PLE), lambda b, t: (0, b, t, 0))] + mixer_specs,
        out_specs=[
            pl.BlockSpec((nb, PROMPT_TILE, D_PAY), lambda b, t: (b, t, 0)),
            pl.BlockSpec((nb, SUBLANES, PROMPT_TILE), lambda b, t: (b, 0, t)),
            _full((CLASS_ROWS, 1)),
            pl.BlockSpec((nb, POOL_HALO, D_POOL), lambda b, t: (b, 0, 0)),
            pl.BlockSpec((nb, CONV_HALO, D_RNN), lambda b, t: (b, 0, 0)),
            pl.BlockSpec((nb, SUBLANES, D_RNN), lambda b, t: (b, 0, 0)),
        ],
        out_shape=[
            jax.ShapeDtypeStruct((bp, seq, D_PAY), F32),
            jax.ShapeDtypeStruct((bp, SUBLANES, seq), F32),
            jax.ShapeDtypeStruct((CLASS_ROWS, 1), F32),
            jax.ShapeDtypeStruct((bp, POOL_HALO, D_POOL), F32),
            jax.ShapeDtypeStruct((bp, CONV_HALO, D_RNN), F32),
            jax.ShapeDtypeStruct((bp, SUBLANES, D_RNN), F32),
        ],
        scratch_shapes=[pltpu.VMEM((nb, POOL_HALO, D_POOL), F32), pltpu.VMEM((nb, CONV_HALO, D_RNN), F32),
                        pltpu.VMEM((nb, 1, D_RNN), F32), pltpu.VMEM((CLASS_ROWS, 1), F32)],
        compiler_params=pltpu.CompilerParams(dimension_semantics=("arbitrary", "arbitrary"),
                                             vmem_limit_bytes=VMEM_LIMIT),
        name="prompt_mixers",
    )(x_prompt, p_prompt, *mixer_params)
    pay_p = pay_p.reshape(n_p, D_PAY)

    xs_tm = x_sample.transpose(1, 0, 2).reshape(n_s, D_MODEL)
    pe_tm = p_sample[0].transpose(1, 0, 2).reshape(n_s, D_PLE)
    pool_buf_tm = state_pool[0].transpose(1, 0, 2)
    conv_buf_tm = state_conv[0].transpose(1, 0, 2)
    sample_in = (xs_tm, pe_tm, pool_buf_tm, conv_buf_tm, state_h[0], cnt_p)
    pay_s, route_s, cnt_all, pool_s, conv_s, h_s = pl.pallas_call(
        _sample_kernel,
        grid=(1,),
        in_specs=[_full(p.shape) for p in sample_in] + mixer_specs,
        out_specs=[_full((n_s, D_PAY)), _full((SUBLANES, n_s)),
                   _full((CLASS_ROWS, 1)), _full((POOL_BUF, bs, D_POOL)),
                   _full((CONV_WIDTH - 1, bs, D_RNN)), _full((bs, D_RNN))],
        out_shape=[
            jax.ShapeDtypeStruct((n_s, D_PAY), F32),
            jax.ShapeDtypeStruct((SUBLANES, n_s), F32),
            jax.ShapeDtypeStruct((CLASS_ROWS, 1), F32),
            jax.ShapeDtypeStruct((POOL_BUF, bs, D_POOL), F32),
            jax.ShapeDtypeStruct((CONV_WIDTH - 1, bs, D_RNN), F32),
            jax.ShapeDtypeStruct((bs, D_RNN), F32),
        ],
        compiler_params=pltpu.CompilerParams(dimension_semantics=("arbitrary",), vmem_limit_bytes=VMEM_LIMIT),
        name="sample_mixers",
    )(*sample_in, *mixer_params)

    tm = MOE_TILE
    n_tiles = n_tok // tm + N_CLASSES + 1
    counts = cnt_all[:N_CLASSES, 0].astype(I32)
    counts_p = cnt_p[:N_CLASSES, 0].astype(I32)
    tiles_c = (counts + tm - 1) // tm
    tile_end = jnp.cumsum(tiles_c)
    tile_start = tile_end - tiles_c
    row_start = tile_start * tm
    cls_all = jnp.concatenate([route_p[:, 0, :].reshape(n_p), route_s[0]]).astype(I32)
    rank_all = jnp.concatenate([route_p[:, 1, :].reshape(n_p), route_s[1]]).astype(I32)
    dest = row_start[cls_all] + rank_all
    tile_id = jnp.arange(n_tiles, dtype=I32)
    tile_cls = jnp.minimum(jnp.sum((tile_id[:, None] >= tile_end[None, :]).astype(I32), axis=1), N_CLASSES - 1)
    used = tile_id < tile_end[-1]
    first_row = (tile_id - tile_start[tile_cls]) * tm
    nv = jnp.where(used, jnp.clip(counts[tile_cls] - first_row, 0, tm), 0).astype(I32)
    nvp = jnp.minimum(jnp.clip(counts_p[tile_cls] - first_row, 0, tm), nv).astype(I32)
    pair_lo = jnp.array([p[0] for p in PAIRS], I32)
    pair_hi = jnp.array([p[1] for p in PAIRS], I32)
    tile_group = tile_cls // len(PAIRS)
    ea = (tile_group * EXPERTS_PER_GROUP + pair_lo[tile_cls % len(PAIRS)]).astype(I32)
    eb = (tile_group * EXPERTS_PER_GROUP + pair_hi[tile_cls % len(PAIRS)]).astype(I32)

    moe_params = (_row(ln2_g[0]), _row(ln2_b[0]), w_pg[0].astype(BF16), _row(b_pg[0]), w_ple[0].astype(BF16),
                  _row(ln3_g[0]), _row(ln3_b[0]))
    up_spec_a = pl.BlockSpec((None, D_MODEL, D_EXPERT), lambda t, d, n, p, a, b: (a[t], 0, 0))
    up_spec_b = pl.BlockSpec((None, D_MODEL, D_EXPERT), lambda t, d, n, p, a, b: (b[t], 0, 0))
    dn_spec_a = pl.BlockSpec((None, D_EXPERT, D_MODEL), lambda t, d, n, p, a, b: (a[t], 0, 0))
    dn_spec_b = pl.BlockSpec((None, D_EXPERT, D_MODEL), lambda t, d, n, p, a, b: (b[t], 0, 0))
    y_p, y_s = pl.pallas_call(
        functools.partial(_moe_kernel, sample_batch=bs, sample_seq=dseq),
        grid_spec=pltpu.PrefetchScalarGridSpec(
            num_scalar_prefetch=5,
            grid=(n_tiles,),
            in_specs=[pl.BlockSpec(memory_space=pl.ANY), pl.BlockSpec(memory_space=pl.ANY),
                      up_spec_a, up_spec_a, dn_spec_a,
                      up_spec_b, up_spec_b, dn_spec_b] + [_full(p.shape) for p in moe_params],
            out_specs=[pl.BlockSpec(memory_space=pl.ANY)] * 2,
            scratch_shapes=[pltpu.SMEM((n_tiles * tm,), I32)]
            + [pltpu.VMEM((tm // SUBLANES, SUBLANES, D_PAY), F32)] * 2
            + [pltpu.VMEM((tm // SUBLANES, SUBLANES, D_MODEL), F32)] * 2
            + [pltpu.VMEM((D_MODEL, D_EXPERT), BF16), pltpu.VMEM((D_MODEL, D_EXPERT), BF16),
               pltpu.VMEM((D_EXPERT, D_MODEL), BF16)] * 2
            + [pltpu.SemaphoreType.DMA((2,)), pltpu.SemaphoreType.DMA((2,))],
        ),
        out_shape=[jax.ShapeDtypeStruct((n_p, D_MODEL), F32), jax.ShapeDtypeStruct((n_s, D_MODEL), F32)],
        compiler_params=pltpu.CompilerParams(dimension_semantics=("arbitrary",), vmem_limit_bytes=VMEM_LIMIT),
        name="moe_embed",
    )(dest, nv, nvp, ea, eb, pay_p, pay_s, w_eg[0], w_eu[0], w_ed[0], w_eg[0], w_eu[0], w_ed[0], *moe_params)

    return (y_p.reshape(bp, seq, D_MODEL), y_s.reshape(bs, dseq, D_MODEL),
            pool_p[None, :, POOL_HALO - POOL_BUF:, :], conv_p[None, :, CONV_HALO - (CONV_WIDTH - 1):, :],
            h_p[None, :, 0, :],
            pool_s.transpose(1, 0, 2)[None], conv_s.transpose(1, 0, 2)[None], h_s[None])
```

```python
import functools

import jax
import jax.numpy as jnp
from jax import lax
from jax.experimental import pallas as pl
from jax.experimental.pallas import tpu as pltpu

F32 = jnp.float32
BF16 = jnp.bfloat16
I32 = jnp.int32

D_MODEL = 1024
D_POOL = 512
D_RNN = 512
POOL_WINDOWS = (2, 4, 8, 16)
POOL_GROUP = 128
POOL_BUF = 15
N_RNN_HEADS = 8
RNN_HEAD = 64
CONV_WIDTH = 4
LRU_C = 8.0
N_EXPERT_GROUPS = 4
EXPERTS_PER_GROUP = 4
N_EXPERTS = 16
D_EXPERT = 512
D_PLE = 256
LN_EPS = 1e-5
DEPTH = 1
ALPHA = (2 * DEPTH) ** 0.25
PAST_LEN = 16384

LANES = 128
SUBLANES = 8
ROUTER_ROWS = 32
PAIRS = ((0, 1), (0, 2), (0, 3), (1, 2), (1, 3), (2, 3))
N_CLASSES = N_EXPERT_GROUPS * len(PAIRS)
CLASS_ROWS = 32
POOL_HALO = 16
CONV_HALO = 8
PROMPT_TILE = 256
PROMPT_STREAMS = 2
MOE_TILE = 384
UNROLL = SUBLANES
UNROLL_LOG2 = 3
D_PAY = D_MODEL + LANES + D_PLE
VMEM_LIMIT = 56 * 1024 * 1024


def _layer_norm(x, g, b):
    mu = jnp.mean(x, axis=-1, keepdims=True)
    xc = x - mu
    var = jnp.mean(xc * xc, axis=-1, keepdims=True)
    return xc * lax.rsqrt(var + LN_EPS) * g + b


def _bdot(a, w):
    return jnp.dot(a.astype(BF16), w, preferred_element_type=F32)


def _gates(xc, gate_w_ref, ba, bx):
    r_parts, i_parts = [], []
    for j in range(D_RNN // LANES):
        o = _bdot(xc[:, LANES * j:LANES * (j + 1)], gate_w_ref[j])
        r_parts.append(o[:, :LANES])
        i_parts.append(o[:, LANES:])
    r = jax.nn.sigmoid(jnp.concatenate(r_parts, axis=-1) + ba)
    i = jax.nn.sigmoid(jnp.concatenate(i_parts, axis=-1) + bx)
    return r, i


def _lru_terms(xc, r, i, lam, first_pos_mask):
    log_a = -LRU_C * r * jax.nn.softplus(-lam)
    a = jnp.exp(log_a)
    mult = jnp.sqrt(jnp.tanh(-log_a) * (a * a + 1.0))
    if first_pos_mask is not None:
        mult = jnp.where(first_pos_mask, 1.0, mult)
    return a, mult * i * xc


def _pool_project(d, pool_w_ref, pool_b, pool_scale):
    outs = []
    for g in range(len(POOL_WINDOWS)):
        outs.append(_bdot(d[:, POOL_GROUP * g:POOL_GROUP * (g + 1)], pool_w_ref[g]))
    return (jnp.concatenate(outs, axis=-1) + pool_b) * pool_scale


def _route(x1, wr_hi_ref, wr_lo_ref, br):
    hi = x1.astype(BF16)
    lo = (x1 - hi.astype(F32)).astype(BF16)
    logits_tm = (jnp.dot(hi, wr_hi_ref[...], preferred_element_type=F32)
                 + jnp.dot(lo, wr_hi_ref[...], preferred_element_type=F32)
                 + jnp.dot(hi, wr_lo_ref[...], preferred_element_type=F32))
    logits = logits_tm.T[:ROUTER_ROWS, :] + br
    lg = [logits[j:j + 1, :] for j in range(N_EXPERT_GROUPS)]
    m = functools.reduce(jnp.maximum, lg)
    ex = [jnp.exp(v - m) for v in lg]
    den = functools.reduce(lambda p, q: p + q, ex)
    gp = [v / den for v in ex]
    g_w, g_idx = gp[0], jnp.zeros_like(gp[0], dtype=I32)
    for j in range(1, N_EXPERT_GROUPS):
        upd = gp[j] > g_w
        g_idx = jnp.where(upd, j, g_idx)
        g_w = jnp.where(upd, gp[j], g_w)
    e_in = []
    for k in range(EXPERTS_PER_GROUP):
        v = jnp.zeros_like(g_w)
        for g in range(N_EXPERT_GROUPS):
            row = N_EXPERT_GROUPS + g * EXPERTS_PER_GROUP + k
            v = v + jnp.where(g_idx == g, logits[row:row + 1, :], 0.0)
        e_in.append(v)
    m = functools.reduce(jnp.maximum, e_in)
    ex = [jnp.exp(v - m) for v in e_in]
    den = functools.reduce(lambda p, q: p + q, ex)
    ep = [v / den for v in ex]
    w0, i0 = ep[0], jnp.zeros_like(g_idx)
    for k in range(1, EXPERTS_PER_GROUP):
        upd = ep[k] > w0
        i0 = jnp.where(upd, k, i0)
        w0 = jnp.where(upd, ep[k], w0)
    w1, i1 = jnp.full_like(w0, -1.0), jnp.zeros_like(g_idx)
    for k in range(EXPERTS_PER_GROUP):
        upd = (ep[k] > w1) & (i0 != k)
        i1 = jnp.where(upd, k, i1)
        w1 = jnp.where(upd, ep[k], w1)
    wsum = w0 + w1
    c0 = g_w * (w0 / wsum)
    c1 = g_w * (w1 / wsum)
    first_is_lo = i0 < i1
    e_lo = jnp.minimum(i0, i1)
    e_hi = jnp.maximum(i0, i1)
    pair = jnp.zeros_like(g_idx)
    for p, (a, b) in enumerate(PAIRS):
        pair = jnp.where((e_lo == a) & (e_hi == b), p, pair)
    cls = g_idx * len(PAIRS) + pair
    return cls, jnp.where(first_is_lo, c0, c1), jnp.where(first_is_lo, c1, c0)


def _rank_in_class(cls, cnt_ref):
    t = cls.shape[1]
    onehot = (lax.broadcasted_iota(I32, (CLASS_ROWS, t), 0) == cls).astype(F32)
    before = (lax.broadcasted_iota(I32, (t, t), 0) < lax.broadcasted_iota(I32, (t, t), 1)).astype(BF16)
    prefix = jnp.dot(onehot.astype(BF16), before, preferred_element_type=F32)
    rank = jnp.sum(onehot * (prefix + cnt_ref[...]), axis=0, keepdims=True)
    cnt_ref[...] = cnt_ref[...] + jnp.sum(onehot, axis=1, keepdims=True)
    return rank


def _write_token_rows(pay_ref, route_ref, x1, pe, cls, rank, c_lo, c_hi):
    t = x1.shape[0]
    meta_t = jnp.concatenate([c_lo, c_hi, jnp.zeros((LANES - 2, t), F32)], axis=0)
    pay_ref[:, :D_MODEL] = x1
    pay_ref[:, D_MODEL:D_MODEL + LANES] = meta_t.T
    pay_ref[:, D_MODEL + LANES:] = pe
    route_ref[...] = jnp.concatenate([cls.astype(F32), rank, jnp.zeros((SUBLANES - 2, t), F32)], axis=0)


def _scan_rows(a, b, h0):
    t, c = a.shape
    sub = lax.broadcasted_iota(I32, (SUBLANES, c), 0)
    keep = {s: sub >= s for s in (1, 2, 4)}
    h, out = h0, []
    for g in range(t // SUBLANES):
        ag = a[SUBLANES * g:SUBLANES * (g + 1), :]
        bg = b[SUBLANES * g:SUBLANES * (g + 1), :]
        for s in (1, 2, 4):
            a_prev = jnp.where(keep[s], pltpu.roll(ag, s, 0), 1.0)
            b_prev = jnp.where(keep[s], pltpu.roll(bg, s, 0), 0.0)
            bg = ag * b_prev + bg
            ag = ag * a_prev
        hg = ag * h + bg
        h = hg[SUBLANES - 1:SUBLANES, :]
        out.append(hg)
    return jnp.concatenate(out, axis=0)


_DONE = object()


def _shift_rows(x, s):
    return pltpu.roll(x, s, 0)


def _prompt_kernel(x_ref, pe_ref, ln_in_g, ln_in_b, w_in, pool_w, pool_b, pool_scale, conv_w, conv_b,
                   gate_w, ba, bx, lam, w_out, ln1_g, ln1_b, wr_hi, wr_lo, br,
                   pay_ref, route_ref, cnt_out_ref, pool_new_ref, conv_new_ref, h_new_ref,
                   pool_halo, conv_halo, h_carry, cnt):
    t_idx = pl.program_id(1)

    @pl.when((pl.program_id(0) == 0) & (t_idx == 0))
    def _():
        cnt[...] = jnp.zeros_like(cnt)

    @pl.when(t_idx == 0)
    def _():
        pool_halo[...] = jnp.zeros_like(pool_halo)
        conv_halo[...] = jnp.zeros_like(conv_halo)
        h_carry[...] = jnp.zeros_like(h_carry)

    tiles = [_prompt_tile(t_idx, x_ref.at[s], pe_ref.at[s], ln_in_g, ln_in_b, w_in, pool_w, pool_b, pool_scale,
                          conv_w, conv_b, gate_w, ba, bx, lam, w_out, ln1_g, ln1_b, wr_hi, wr_lo, br,
                          pay_ref.at[s], route_ref.at[s], pool_new_ref.at[s], conv_new_ref.at[s],
                          h_new_ref.at[s], pool_halo.at[s], conv_halo.at[s], h_carry.at[s], cnt)
             for s in range(x_ref.shape[0])]
    while tiles:
        tiles = [g for g in tiles if next(g, _DONE) is not _DONE]
    cnt_out_ref[...] = cnt[...]


def _prompt_tile(t_idx, x_ref, pe_ref, ln_in_g, ln_in_b, w_in, pool_w, pool_b, pool_scale, conv_w, conv_b,
                 gate_w, ba, bx, lam, w_out, ln1_g, ln1_b, wr_hi, wr_lo, br,
                 pay_ref, route_ref, pool_new_ref, conv_new_ref, h_new_ref,
                 pool_halo, conv_halo, h_carry, cnt):
    tt = x_ref.shape[0]
    xn = _layer_norm(x_ref[...], ln_in_g[...], ln_in_b[...])
    proj = _bdot(xn, w_in[...])
    yield
    u = proj[:, :D_POOL]
    xr = proj[:, D_POOL:D_POOL + D_RNN]
    gate = proj[:, D_POOL + D_RNN:]

    row = lax.broadcasted_iota(I32, (tt, 1), 0)
    pos = t_idx * tt + row

    e = jnp.concatenate([pool_halo[...], u], axis=0)
    s = e + _shift_rows(e, 1)
    sums = [s[:, :POOL_GROUP]]
    s = s[:, POOL_GROUP:]
    for step in (2, 4, 8):
        s = s + _shift_rows(s, step)
        sums.append(s[:, :POOL_GROUP])
        s = s[:, POOL_GROUP:]
    d_parts = []
    for g, win in enumerate(POOL_WINDOWS):
        cnt_w = jnp.minimum(pos + 1, win).astype(F32)
        d_parts.append(sums[g][POOL_HALO:, :] / cnt_w - u[:, POOL_GROUP * g:POOL_GROUP * (g + 1)])
    pool_out = _pool_project(jnp.concatenate(d_parts, axis=-1), pool_w, pool_b[...], pool_scale[...])
    pool_halo[...] = u[tt - POOL_HALO:, :]
    yield

    ec = jnp.concatenate([conv_halo[...], xr], axis=0)
    xc = conv_b[...] + _shift_rows(ec, 3) * conv_w[0:1, :]
    xc = xc + _shift_rows(ec, 2) * conv_w[1:2, :]
    xc = xc + _shift_rows(ec, 1) * conv_w[2:3, :]
    xc = (xc + ec * conv_w[3:4, :])[CONV_HALO:, :]
    conv_halo[...] = xr[tt - CONV_HALO:, :]

    r, i = _gates(xc, gate_w, ba[...], bx[...])
    yield
    a, b = _lru_terms(xc, r, i, lam[...], pos == 0)
    h = _scan_rows(a, b, h_carry[...])
    h_carry[...] = h[tt - 1:tt, :]
    rnn_out = h * jax.nn.gelu(gate)
    yield

    mix = _bdot(jnp.concatenate([pool_out, rnn_out], axis=-1), w_out[...])
    x1 = _layer_norm(ALPHA * xn + mix, ln1_g[...], ln1_b[...])
    yield
    cls, c_lo, c_hi = _route(x1, wr_hi, wr_lo, br[...])
    rank = _rank_in_class(cls, cnt)
    _write_token_rows(pay_ref, route_ref, x1, pe_ref[...], cls, rank, c_lo, c_hi)
    pool_new_ref[...] = u[tt - POOL_HALO:, :]
    conv_new_ref[...] = xr[tt - CONV_HALO:, :]
    h_new_ref[...] = jnp.broadcast_to(h[tt - 1:tt, :], h_new_ref.shape)


def _sample_kernel(x_ref, pe_ref, pool_buf_ref, conv_buf_ref, h0_ref, cnt_in_ref,
                   ln_in_g, ln_in_b, w_in, pool_w, pool_b, pool_scale, conv_w, conv_b,
                   gate_w, ba, bx, lam, w_out, ln1_g, ln1_b, wr_hi, wr_lo, br,
                   pay_ref, route_ref, cnt_out_ref, pool_new_ref, conv_new_ref, h_new_ref):
    nb = h0_ref.shape[0]
    t_len = x_ref.shape[0] // nb
    xn = _layer_norm(x_ref[...], ln_in_g[...], ln_in_b[...])
    proj = _bdot(xn, w_in[...])
    u = proj[:, :D_POOL]
    xr = proj[:, D_POOL:D_POOL + D_RNN]
    gate = proj[:, D_POOL + D_RNN:]

    f = [pool_buf_ref[k] for k in range(POOL_BUF)] + [u[nb * t:nb * (t + 1), :] for t in range(t_len)]
    n_rows = len(f)
    s = [f[0]] + [f[k] + f[k - 1] for k in range(1, n_rows)]
    sums = [[v[:, :POOL_GROUP] for v in s]]
    s = [v[:, POOL_GROUP:] for v in s]
    for step in (2, 4, 8):
        s = [s[k] + s[k - step] if k >= step else s[k] for k in range(n_rows)]
        sums.append([v[:, :POOL_GROUP] for v in s])
        s = [v[:, POOL_GROUP:] for v in s]
    d_rows = []
    for t in range(t_len):
        parts = []
        for g, win in enumerate(POOL_WINDOWS):
            cnt_w = float(min(PAST_LEN + t + 1, win))
            parts.append(sums[g][POOL_BUF + t] / cnt_w - f[POOL_BUF + t][:, POOL_GROUP * g:POOL_GROUP * (g + 1)])
        d_rows.append(jnp.concatenate(parts, axis=-1))
    pool_out = _pool_project(jnp.concatenate(d_rows, axis=0), pool_w, pool_b[...], pool_scale[...])
    for k in range(POOL_BUF):
        pool_new_ref[k] = f[n_rows - POOL_BUF + k]

    gbuf = [conv_buf_ref[k] for k in range(CONV_WIDTH - 1)] + [xr[nb * t:nb * (t + 1), :] for t in range(t_len)]
    xc_rows = []
    for t in range(t_len):
        v = conv_b[...] + gbuf[t] * conv_w[0:1, :]
        for k in range(1, CONV_WIDTH):
            v = v + gbuf[t + k] * conv_w[k:k + 1, :]
        xc_rows.append(v)
    xc = jnp.concatenate(xc_rows, axis=0)
    for k in range(CONV_WIDTH - 1):
        conv_new_ref[k] = gbuf[len(gbuf) - (CONV_WIDTH - 1) + k]

    r, i = _gates(xc, gate_w, ba[...], bx[...])
    a, b = _lru_terms(xc, r, i, lam[...], None)
    h = h0_ref[...]
    h_rows = []
    for t in range(t_len):
        h = a[nb * t:nb * (t + 1), :] * h + b[nb * t:nb * (t + 1), :]
        h_rows.append(h)
    h_new_ref[...] = h
    rnn_out = jnp.concatenate(h_rows, axis=0) * jax.nn.gelu(gate)

    mix = _bdot(jnp.concatenate([pool_out, rnn_out], axis=-1), w_out[...])
    x1 = _layer_norm(ALPHA * xn + mix, ln1_g[...], ln1_b[...])
    cls, c_lo, c_hi = _route(x1, wr_hi, wr_lo, br[...])
    cnt_out_ref[...] = cnt_in_ref[...]
    rank = _rank_in_class(cls, cnt_out_ref)
    _write_token_rows(pay_ref, route_ref, x1, pe_ref[...], cls, rank, c_lo, c_hi)


def _moe_kernel(dest_ref, nv_ref, nvp_ref, ea_ref, eb_ref,
                payp_hbm, pays_hbm, wg_a, wu_a, wd_a, wg_b, wu_b, wd_b,
                ln2_g, ln2_b, w_pg, b_pg, w_ple, ln3_g, ln3_b,
                yp_hbm, ys_hbm,
                src, gbuf0, gbuf1, obuf0, obuf1, wga_bf, wua_bf, wda_bf, wgb_bf, wub_bf, wdb_bf, gsem, ssem,
                *, sample_batch, sample_seq):
    t = pl.program_id(0)
    wbf_a, wbf_b = (wga_bf, wua_bf, wda_bf), (wgb_bf, wub_bf, wdb_bf)

    prev = jnp.maximum(t - 1, 0)
    for e_ref, w_f32, w_bf in ((ea_ref, (wg_a, wu_a, wd_a), wbf_a), (eb_ref, (wg_b, wu_b, wd_b), wbf_b)):
        @pl.when((t == 0) | (e_ref[t] != e_ref[prev]))
        def _(w_f32=w_f32, w_bf=w_bf):
            for src_ref, dst_ref in zip(w_f32, w_bf):
                dst_ref[...] = src_ref[...].astype(BF16)
    wg_a, wu_a, wd_a = wbf_a
    wg_b, wu_b, wd_b = wbf_b

    n_tiles = pl.num_programs(0)
    n_groups = gbuf0.shape[0]
    tm = n_groups * SUBLANES
    n_tok = dest_ref.shape[0]
    n_prompt = payp_hbm.shape[0]
    log2_batch = sample_batch.bit_length() - 1
    gbufs, obufs = (gbuf0, gbuf1), (obuf0, obuf1)

    def buf_row(buf, j):
        if isinstance(j, int):
            return buf.at[j // SUBLANES, pl.ds(j % SUBLANES, 1), :]
        return buf.at[lax.shift_right_logical(j, UNROLL_LOG2), pl.ds(j & (SUBLANES - 1), 1), :]

    def for_each_row(lo, hi, per_row):
        def row_body(j, c):
            per_row(j)
            return c
        lax.fori_loop(lo, hi, row_body, 0)

    def gather_prompt_row(tile, par, j):
        pltpu.make_async_copy(payp_hbm.at[pl.ds(src[tile * tm + j], 1), :], buf_row(gbufs[par], j),
                              gsem.at[par]).start()

    def gather_sample_rows(tile, par):
        def from_sample(j):
            pltpu.make_async_copy(pays_hbm.at[pl.ds(src[tile * tm + j] - n_prompt, 1), :], buf_row(gbufs[par], j),
                                  gsem.at[par]).start()
        for_each_row(nvp_ref[tile], nv_ref[tile], from_sample)

    def scatter_prompt_row(tile, par, j):
        pltpu.make_async_copy(buf_row(obufs[par], j), yp_hbm.at[pl.ds(src[tile * tm + j], 1), :],
                              ssem.at[par]).start()

    def scatter_sample_rows(tile, par):
        def to_sample(j):
            s = src[tile * tm + j] - n_prompt
            r = (s & (sample_batch - 1)) * sample_seq + lax.shift_right_logical(s, log2_batch)
            pltpu.make_async_copy(buf_row(obufs[par], j), ys_hbm.at[pl.ds(r, 1), :], ssem.at[par]).start()
        for_each_row(nvp_ref[tile], nv_ref[tile], to_sample)

    def wait_rows(rows_hbm, n, sem):
        n8 = pl.multiple_of(n & -SUBLANES, SUBLANES)

        @pl.when(n8 > 0)
        def _():
            rows = rows_hbm.at[pl.ds(0, n8), :]
            pltpu.make_async_copy(rows, rows, sem).wait()

        def one(j):
            row = rows_hbm.at[pl.ds(0, 1), :]
            pltpu.make_async_copy(row, row, sem).wait()
        for_each_row(n8, n, one)

    def compute(par, overlapped_issue):
        gbuf, obuf = gbufs[par], obufs[par]
        xp = jnp.concatenate([gbuf[g] for g in range(n_groups)], axis=0)
        x1 = xp[:, :D_MODEL]
        meta = xp[:, D_MODEL:D_MODEL + LANES]
        pe = xp[:, D_MODEL + LANES:]
        xb = x1.astype(BF16)
        overlapped_issue()

        def expert(wg, wu, wd):
            hid = jax.nn.silu(jnp.dot(xb, wg[...], preferred_element_type=F32)) * jnp.dot(
                xb, wu[...], preferred_element_type=F32)
            return _bdot(hid, wd[...])

        moe = meta[:, 0:1] * expert(wg_a, wu_a, wd_a)
        moe = moe + meta[:, 1:2] * expert(wg_b, wu_b, wd_b)
        x2 = _layer_norm(ALPHA * x1 + moe, ln2_g[...], ln2_b[...])
        gate_p = jax.nn.sigmoid(_bdot(x2, w_pg[...]) + b_pg[...])
        ple = _bdot(pe, w_ple[...]) * gate_p
        y = _layer_norm(ALPHA * x2 + ple, ln3_g[...], ln3_b[...])
        for g in range(n_groups):
            obuf[g] = y[SUBLANES * g:SUBLANES * (g + 1), :]

    @pl.when(t == 0)
    def _():
        gbuf0[...] = jnp.zeros_like(gbuf0)
        gbuf1[...] = jnp.zeros_like(gbuf1)

        def fill(g, c):
            for k in range(UNROLL):
                i = g * UNROLL + k
                src[dest_ref[i]] = i
            return c
        lax.fori_loop(0, n_tok // UNROLL, fill, 0)
        for_each_row(0, nvp_ref[0], functools.partial(gather_prompt_row, 0, 0))
        gather_sample_rows(0, 0)

    nxt = jnp.minimum(t + 1, n_tiles - 1)
    prv = jnp.maximum(t - 1, 0)
    n_gather_next = jnp.where(t + 1 < n_tiles, nvp_ref[nxt], 0)
    n_scatter_prev = jnp.where(t >= 1, nvp_ref[prv], 0)

    def step(par):
        oth = 1 - par
        wait_rows(payp_hbm, nv_ref[t], gsem.at[par])

        @pl.when(t >= 2)
        def _():
            wait_rows(yp_hbm, nv_ref[jnp.maximum(t - 2, 0)], ssem.at[par])

        def overlapped_issue():
            for j in range(tm):
                @pl.when(j < n_gather_next)
                def _(j=j):
                    gather_prompt_row(nxt, oth, j)

                @pl.when(j < n_scatter_prev)
                def _(j=j):
                    scatter_prompt_row(prv, oth, j)

        @pl.when(nv_ref[t] > 0)
        def _():
            compute(par, overlapped_issue)

        @pl.when(nv_ref[t] <= 0)
        def _():
            for_each_row(0, n_gather_next, functools.partial(gather_prompt_row, nxt, oth))
            for_each_row(0, n_scatter_prev, functools.partial(scatter_prompt_row, prv, oth))

        @pl.when(t + 1 < n_tiles)
        def _():
            gather_sample_rows(nxt, oth)

        @pl.when(t >= 1)
        def _():
            scatter_sample_rows(prv, oth)

        @pl.when(t == n_tiles - 1)
        def _():
            wait_rows(yp_hbm, nv_ref[prv], ssem.at[oth])

    @pl.when(t % 2 == 0)
    def _():
        step(0)

    @pl.when(t % 2 == 1)
    def _():
        step(1)


def _full(shape):
    nd = len(shape)
    return pl.BlockSpec(shape, lambda *_: (0,) * nd)


def _row(v):
    return v.reshape(1, -1).astype(F32)


def kernel(x_prompt, x_sample, state_pool, state_conv, state_h, p_prompt, p_sample, ln_in_g, ln_in_b, w_in,
           pool_w, pool_b, pool_scale, conv_w, conv_b, rg_wa, rg_ba, rg_wx, rg_bx, rg_lambda, w_out, ln1_g,
           ln1_b, w_rg, b_rg, w_re, b_re, w_eg, w_eu, w_ed, ln2_g, ln2_b, w_pg, b_pg, w_ple, ln3_g, ln3_b):
    bp, seq, _ = x_prompt.shape
    bs, dseq, _ = x_sample.shape
    n_p, n_s = bp * seq, bs * dseq
    n_tok = n_p + n_s
    assert seq % PROMPT_TILE == 0 and bp % PROMPT_STREAMS == 0 and n_tok % MOE_TILE == 0
    assert bs & (bs - 1) == 0, "sample batch must be a power of two"

    w_in_b = w_in[0].astype(BF16)
    w_out_b = w_out[0].astype(BF16)
    pool_w_b = pool_w[0].astype(BF16)
    zero = jnp.zeros((RNN_HEAD, RNN_HEAD), F32)

    def pair_block(w, j):
        top = jnp.concatenate([w[2 * j], zero], axis=1)
        bot = jnp.concatenate([zero, w[2 * j + 1]], axis=1)
        return jnp.concatenate([top, bot], axis=0)

    gate_w = jnp.stack([jnp.concatenate([pair_block(rg_wa[0], j), pair_block(rg_wx[0], j)], axis=1)
                        for j in range(N_RNN_HEADS // 2)]).astype(BF16)
    wr = jnp.concatenate([w_rg[0], w_re[0],
                          jnp.zeros((D_MODEL, LANES - N_EXPERT_GROUPS - N_EXPERTS), F32)], axis=1)
    wr_hi = wr.astype(BF16)
    wr_lo = (wr - wr_hi.astype(F32)).astype(BF16)
    br = jnp.concatenate([b_rg[0], b_re[0],
                          jnp.zeros((ROUTER_ROWS - N_EXPERT_GROUPS - N_EXPERTS,), F32)]).reshape(ROUTER_ROWS, 1)

    mixer_params = (_row(ln_in_g), _row(ln_in_b), w_in_b, pool_w_b, _row(pool_b[0]), _row(pool_scale[0]),
                    conv_w[0], _row(conv_b[0]), gate_w, _row(rg_ba[0]), _row(rg_bx[0]), _row(rg_lambda[0]),
                    w_out_b, _row(ln1_g[0]), _row(ln1_b[0]), wr_hi, wr_lo, br)
    mixer_specs = [_full(p.shape) for p in mixer_params]

    n_t = seq // PROMPT_TILE
    nb = PROMPT_STREAMS
    pay_p, route_p, cnt_p, pool_p, conv_p, h_p = pl.pallas_call(
        _prompt_kernel,
        grid=(bp // nb, n_t),
        in_specs=[pl.BlockSpec((nb, PROMPT_TILE, D_MODEL), lambda b, t: (b, t, 0)),
                  pl.BlockSpec((None, nb, PROMPT_TILE, D_PLE), lambda b, t: (0, b, t, 0))] + mixer_specs,
        out_specs=[
            pl.BlockSpec((nb, PROMPT_TILE, D_PAY), lambda b, t: (b, t, 0)),
            pl.BlockSpec((nb, SUBLANES, PROMPT_TILE), lambda b, t: (b, 0, t)),
            _full((CLASS_ROWS, 1)),
            pl.BlockSpec((nb, POOL_HALO, D_POOL), lambda b, t: (b, 0, 0)),
            pl.BlockSpec((nb, CONV_HALO, D_RNN), lambda b, t: (b, 0, 0)),
            pl.BlockSpec((nb, SUBLANES, D_RNN), lambda b, t: (b, 0, 0)),
        ],
        out_shape=[
            jax.ShapeDtypeStruct((bp, seq, D_PAY), F32),
            jax.ShapeDtypeStruct((bp, SUBLANES, seq), F32),
            jax.ShapeDtypeStruct((CLASS_ROWS, 1), F32),
            jax.ShapeDtypeStruct((bp, POOL_HALO, D_POOL), F32),
            jax.ShapeDtypeStruct((bp, CONV_HALO, D_RNN), F32),
            jax.ShapeDtypeStruct((bp, SUBLANES, D_RNN), F32),
        ],
        scratch_shapes=[pltpu.VMEM((nb, POOL_HALO, D_POOL), F32), pltpu.VMEM((nb, CONV_HALO, D_RNN), F32),
                        pltpu.VMEM((nb, 1, D_RNN), F32), pltpu.VMEM((CLASS_ROWS, 1), F32)],
        compiler_params=pltpu.CompilerParams(dimension_semantics=("arbitrary", "arbitrary"),
                                             vmem_limit_bytes=VMEM_LIMIT),
        name="prompt_mixers",
    )(x_prompt, p_prompt, *mixer_params)
    pay_p = pay_p.reshape(n_p, D_PAY)

    xs_tm = x_sample.transpose(1, 0, 2).reshape(n_s, D_MODEL)
    pe_tm = p_sample[0].transpose(1, 0, 2).reshape(n_s, D_PLE)
    pool_buf_tm = state_pool[0].transpose(1, 0, 2)
    conv_buf_tm = state_conv[0].transpose(1, 0, 2)
    sample_in = (xs_tm, pe_tm, pool_buf_tm, conv_buf_tm, state_h[0], cnt_p)
    pay_s, route_s, cnt_all, pool_s, conv_s, h_s = pl.pallas_call(
        _sample_kernel,
        grid=(1,),
        in_specs=[_full(p.shape) for p in sample_in] + mixer_specs,
        out_specs=[_full((n_s, D_PAY)), _full((SUBLANES, n_s)),
                   _full((CLASS_ROWS, 1)), _full((POOL_BUF, bs, D_POOL)),
                   _full((CONV_WIDTH - 1, bs, D_RNN)), _full((bs, D_RNN))],
        out_shape=[
            jax.ShapeDtypeStruct((n_s, D_PAY), F32),
            jax.ShapeDtypeStruct((SUBLANES, n_s), F32),
            jax.ShapeDtypeStruct((CLASS_ROWS, 1), F32),
            jax.ShapeDtypeStruct((POOL_BUF, bs, D_POOL), F32),
            jax.ShapeDtypeStruct((CONV_WIDTH - 1, bs, D_RNN), F32),
            jax.ShapeDtypeStruct((bs, D_RNN), F32),
        ],
        compiler_params=pltpu.CompilerParams(dimension_semantics=("arbitrary",), vmem_limit_bytes=VMEM_LIMIT),
        name="sample_mixers",
    )(*sample_in, *mixer_params)

    tm = MOE_TILE
    n_tiles = n_tok // tm + N_CLASSES + 1
    counts = cnt_all[:N_CLASSES, 0].astype(I32)
    counts_p = cnt_p[:N_CLASSES, 0].astype(I32)
    tiles_c = (counts + tm - 1) // tm
    tile_end = jnp.cumsum(tiles_c)
    tile_start = tile_end - tiles_c
    row_start = tile_start * tm
    cls_all = jnp.concatenate([route_p[:, 0, :].reshape(n_p), route_s[0]]).astype(I32)
    rank_all = jnp.concatenate([route_p[:, 1, :].reshape(n_p), route_s[1]]).astype(I32)
    class_ids = jnp.arange(N_CLASSES, dtype=I32)
    dest = rank_all + jnp.sum(jnp.where(cls_all[:, None] == class_ids[None, :], row_start[None, :], 0), axis=1)
    tile_id = jnp.arange(n_tiles, dtype=I32)
    tile_cls = jnp.minimum(jnp.sum((tile_id[:, None] >= tile_end[None, :]).astype(I32), axis=1), N_CLASSES - 1)
    used = tile_id < tile_end[-1]
    first_row = (tile_id - tile_start[tile_cls]) * tm
    nv = jnp.where(used, jnp.clip(counts[tile_cls] - first_row, 0, tm), 0).astype(I32)
    nvp = jnp.minimum(jnp.clip(counts_p[tile_cls] - first_row, 0, tm), nv).astype(I32)
    pair_lo = jnp.array([p[0] for p in PAIRS], I32)
    pair_hi = jnp.array([p[1] for p in PAIRS], I32)
    tile_group = tile_cls // len(PAIRS)
    ea = (tile_group * EXPERTS_PER_GROUP + pair_lo[tile_cls % len(PAIRS)]).astype(I32)
    eb = (tile_group * EXPERTS_PER_GROUP + pair_hi[tile_cls % len(PAIRS)]).astype(I32)

    moe_params = (_row(ln2_g[0]), _row(ln2_b[0]), w_pg[0].astype(BF16), _row(b_pg[0]), w_ple[0].astype(BF16),
                  _row(ln3_g[0]), _row(ln3_b[0]))
    up_spec_a = pl.BlockSpec((None, D_MODEL, D_EXPERT), lambda t, d, n, p, a, b: (a[t], 0, 0))
    up_spec_b = pl.BlockSpec((None, D_MODEL, D_EXPERT), lambda t, d, n, p, a, b: (b[t], 0, 0))
    dn_spec_a = pl.BlockSpec((None, D_EXPERT, D_MODEL), lambda t, d, n, p, a, b: (a[t], 0, 0))
    dn_spec_b = pl.BlockSpec((None, D_EXPERT, D_MODEL), lambda t, d, n, p, a, b: (b[t], 0, 0))
    y_p, y_s = pl.pallas_call(
        functools.partial(_moe_kernel, sample_batch=bs, sample_seq=dseq),
        grid_spec=pltpu.PrefetchScalarGridSpec(
            num_scalar_prefetch=5,
            grid=(n_tiles,),
            in_specs=[pl.BlockSpec(memory_space=pl.ANY), pl.BlockSpec(memory_space=pl.ANY),
                      up_spec_a, up_spec_a, dn_spec_a,
                      up_spec_b, up_spec_b, dn_spec_b] + [_full(p.shape) for p in moe_params],
            out_specs=[pl.BlockSpec(memory_space=pl.ANY)] * 2,
            scratch_shapes=[pltpu.SMEM((n_tiles * tm,), I32)]
            + [pltpu.VMEM((tm // SUBLANES, SUBLANES, D_PAY), F32)] * 2
            + [pltpu.VMEM((tm // SUBLANES, SUBLANES, D_MODEL), F32)] * 2
            + [pltpu.VMEM((D_MODEL, D_EXPERT), BF16), pltpu.VMEM((D_MODEL, D_EXPERT), BF16),
               pltpu.VMEM((D_EXPERT, D_MODEL), BF16)] * 2
            + [pltpu.SemaphoreType.DMA((2,)), pltpu.SemaphoreType.DMA((2,))],
        ),
        out_shape=[jax.ShapeDtypeStruct((n_p, D_MODEL), F32), jax.ShapeDtypeStruct((n_s, D_MODEL), F32)],
        compiler_params=pltpu.CompilerParams(dimension_semantics=("arbitrary",), vmem_limit_bytes=VMEM_LIMIT),
        name="moe_embed",
    )(dest, nv, nvp, ea, eb, pay_p, pay_s, w_eg[0], w_eu[0], w_ed[0], w_eg[0], w_eu[0], w_ed[0], *moe_params)

    return (y_p.reshape(bp, seq, D_MODEL), y_s.reshape(bs, dseq, D_MODEL),
            pool_p[None, :, POOL_HALO - POOL_BUF:, :], conv_p[None, :, CONV_HALO - (CONV_WIDTH - 1):, :],
            h_p[None, :, 0, :],
            pool_s.transpose(1, 0, 2)[None], conv_s.transpose(1, 0, 2)[None], h_s[None])
```
